```python
import math
import jax, jax.numpy as jnp
from jax import lax
import numpy as np

D_MODEL = 1024
BATCH = 32
SEQ = 2048
DEPTH = 1

HEAD_DIM = 64
MOBA_HEADS = 8
MOBA_WIDTH = MOBA_HEADS * HEAD_DIM
MOBA_BLOCK = 256
MOBA_TOPK = 3
MOBA_Q_CHUNK = 16
DIFF_HEADS = 4
DIFF_QK_DIM = HEAD_DIM
DIFF_V_DIM = 2 * HEAD_DIM
DIFF_WIDTH = DIFF_HEADS * DIFF_V_DIM
MIX_WIDTH = MOBA_WIDTH + DIFF_WIDTH
IN_SPLITS = (MOBA_WIDTH, MOBA_WIDTH, MOBA_WIDTH,
             DIFF_HEADS * 2 * DIFF_QK_DIM, DIFF_HEADS * 2 * DIFF_QK_DIM, DIFF_WIDTH)
IN_COLS = sum(IN_SPLITS)
ATTN_Q_BLOCK = 128
ROPE_THETA = 10000.0
EPS = 1e-6
NEG_INF = -1e30
PEER_HEADS = 8
PEER_NKEYS = 128
PEER_EXPERTS = PEER_NKEYS * PEER_NKEYS
PEER_DK = 256
PEER_HALF = PEER_DK // 2
PEER_TOPK = 16
PEER_TOK_CHUNK = 128

kernel_name = "hymba_moba_diffattn_peer_layer"


def rms_norm(x, gain):
    xf = x.astype(jnp.float32)
    y = xf * lax.rsqrt(jnp.mean(xf * xf, axis=-1, keepdims=True) + EPS)
    return (y * gain.astype(jnp.float32)).astype(x.dtype)


def rope_tables(seq):
    pos = jnp.arange(seq, dtype=jnp.float32)
    inv = 1.0 / (ROPE_THETA ** (jnp.arange(0, HEAD_DIM, 2, dtype=jnp.float32) / HEAD_DIM))
    ang = pos[:, None] * inv[None, :]
    return jnp.cos(ang), jnp.sin(ang)


def apply_rope(x, cos, sin):
    x1, x2 = jnp.split(x, 2, axis=-1)
    c = cos.astype(x.dtype)
    s = sin.astype(x.dtype)
    return jnp.concatenate([x1 * c - x2 * s, x2 * c + x1 * s], axis=-1)


def moba_attention(q, k, v):
    B, H, S, Dh = q.shape
    nb = -(-S // MOBA_BLOCK)
    pad = nb * MOBA_BLOCK - S
    kp = jnp.pad(k, ((0, 0), (0, 0), (0, pad), (0, 0)))
    vp = jnp.pad(v, ((0, 0), (0, 0), (0, pad), (0, 0)))
    k_blocks = kp.reshape(B, H, nb, MOBA_BLOCK, Dh)
    v_blocks = vp.reshape(B, H, nb, MOBA_BLOCK, Dh)
    k_mean = jnp.mean(k_blocks.astype(jnp.float32), axis=3)
    gate = jnp.einsum('bhsd,bhnd->bhsn', q.astype(jnp.float32), k_mean)
    q_block = jnp.arange(S) // MOBA_BLOCK
    past = jnp.arange(nb)[None, :] < q_block[:, None]
    gate = jnp.where(past, gate, NEG_INF)
    k_eff = min(MOBA_TOPK, nb)
    _, sel = lax.top_k(gate, k_eff)
    valid = sel < q_block[:, None]

    n_chunks = S // MOBA_Q_CHUNK

    def to_chunks(t):
        t = t.reshape(B, H, n_chunks, MOBA_Q_CHUNK, *t.shape[3:])
        return jnp.moveaxis(t, 2, 0)

    b_idx = jnp.arange(B)[:, None, None, None]
    h_idx = jnp.arange(H)[None, :, None, None]
    scale = Dh ** -0.5
    key_off = jnp.arange(MOBA_BLOCK)

    def chunk_fn(args):
        c, qc, selc, validc = args
        q_pos = c * MOBA_Q_CHUNK + jnp.arange(MOBA_Q_CHUNK)
        blk = (c * MOBA_Q_CHUNK) // MOBA_BLOCK
        k_sel = k_blocks[b_idx, h_idx, selc]
        v_sel = v_blocks[b_idx, h_idx, selc]
        s_sel = jnp.einsum('bhcd,bhckld->bhckl', qc, k_sel).astype(jnp.float32) * scale
        s_sel = jnp.where(validc[..., None], s_sel, NEG_INF)
        k_own = lax.dynamic_slice_in_dim(kp, blk * MOBA_BLOCK, MOBA_BLOCK, axis=2)
        v_own = lax.dynamic_slice_in_dim(vp, blk * MOBA_BLOCK, MOBA_BLOCK, axis=2)
        s_own = jnp.einsum('bhcd,bhld->bhcl', qc, k_own).astype(jnp.float32) * scale
        own_pos = blk * MOBA_BLOCK + key_off
        s_own = jnp.where(own_pos[None, :] <= q_pos[:, None], s_own, NEG_INF)
        n_sel = k_eff * MOBA_BLOCK
        logits = jnp.concatenate([s_sel.reshape(B, H, MOBA_Q_CHUNK, n_sel), s_own], axis=-1)
        p = jax.nn.softmax(logits, axis=-1).astype(v.dtype)
        p_sel = p[..., :n_sel].reshape(B, H, MOBA_Q_CHUNK, k_eff, MOBA_BLOCK)
        p_own = p[..., n_sel:]
        return (jnp.einsum('bhckl,bhckld->bhcd', p_sel, v_sel)
                + jnp.einsum('bhcl,bhld->bhcd', p_own, v_own))

    outs = lax.map(chunk_fn, (jnp.arange(n_chunks), to_chunks(q), to_chunks(sel), to_chunks(valid)))
    return jnp.moveaxis(outs, 0, 2).reshape(B, H, S, Dh)


def diff_attention(q, k, v, lam):
    B, H, _, S, Dk = q.shape
    Dv = v.shape[-1]
    nqb = S // ATTN_Q_BLOCK
    scale = Dk ** -0.5
    key_pos = jnp.arange(S)
    qb = jnp.moveaxis(q.reshape(B, H, 2, nqb, ATTN_Q_BLOCK, Dk), 3, 0)

    def block_fn(args):
        i, qi = args
        q_pos = i * ATTN_Q_BLOCK + jnp.arange(ATTN_Q_BLOCK)
        s = jnp.einsum('bhmqd,bhmkd->bhmqk', qi, k).astype(jnp.float32) * scale
        s = jnp.where(key_pos[None, :] <= q_pos[:, None], s, NEG_INF)
        p = jax.nn.softmax(s, axis=-1)
        a = p[:, :, 0] - lam * p[:, :, 1]
        return jnp.einsum('bhqk,bhkd->bhqd', a.astype(v.dtype), v)

    outs = lax.map(block_fn, (jnp.arange(nqb), qb))
    return jnp.moveaxis(outs, 0, 2).reshape(B, H, S, Dv)


def peer_ffn(x, w_query, sub_keys, expert_down, expert_up):
    B, S, D = x.shape
    T = B * S
    n_chunks = T // PEER_TOK_CHUNK
    xt = x.reshape(n_chunks, PEER_TOK_CHUNK, D)

    def chunk_fn(xc):
        C = xc.shape[0]
        q = (xc @ w_query).reshape(C, PEER_HEADS, 2, PEER_HALF)
        sc = jnp.einsum('thpd,hpnd->thpn', q, sub_keys).astype(jnp.float32)
        top_s, top_i = lax.top_k(sc, PEER_TOPK)
        cand_s = (top_s[:, :, 0, :, None] + top_s[:, :, 1, None, :]).reshape(C, PEER_HEADS, PEER_TOPK * PEER_TOPK)
        cand_i = (top_i[:, :, 0, :, None] * PEER_NKEYS + top_i[:, :, 1, None, :]).reshape(C, PEER_HEADS, PEER_TOPK * PEER_TOPK)
        best_s, best_pos = lax.top_k(cand_s, PEER_TOPK)
        experts = jnp.take_along_axis(cand_i, best_pos, axis=-1)
        gates = jax.nn.softmax(best_s, axis=-1)
        u = expert_down[experts]
        vv = expert_up[experts]
        act = jax.nn.gelu(jnp.einsum('cd,chkd->chk', xc, u).astype(jnp.float32), approximate=False)
        w = (gates * act).astype(xc.dtype)
        return jnp.einsum('chk,chkd->cd', w, vv)

    out = lax.map(chunk_fn, xt)
    return out.reshape(B, S, D)


def setup_inputs(seed: int = 0) -> dict:
    key = jax.random.key(seed)
    ks = jax.random.split(key, 20)
    f32 = jnp.float32

    def nrm(k, shape, scale):
        return jax.random.normal(k, shape, f32) * scale

    def gain(k, shape):
        return 1.0 + 0.01 * jax.random.normal(k, shape, f32)

    L = DEPTH
    return {
        "x": jax.random.normal(ks[0], (BATCH, SEQ, D_MODEL), f32),
        "attn_norm": gain(ks[1], (L, D_MODEL)),
        "w_in": nrm(ks[2], (L, D_MODEL, IN_COLS), D_MODEL ** -0.5),
        "q_norm_moba": gain(ks[3], (L, HEAD_DIM)),
        "k_norm_moba": gain(ks[4], (L, HEAD_DIM)),
        "q_norm_diff": gain(ks[5], (L, 2, DIFF_QK_DIM)),
        "k_norm_diff": gain(ks[6], (L, 2, DIFF_QK_DIM)),
        "lambda_q1": nrm(ks[7], (L, DIFF_QK_DIM), 0.1),
        "lambda_k1": nrm(ks[8], (L, DIFF_QK_DIM), 0.1),
        "lambda_q2": nrm(ks[9], (L, DIFF_QK_DIM), 0.1),
        "lambda_k2": nrm(ks[10], (L, DIFF_QK_DIM), 0.1),
        "moba_out_gain": gain(ks[11], (L, MOBA_HEADS, HEAD_DIM)),
        "diff_out_gain": gain(ks[12], (L, DIFF_HEADS, DIFF_V_DIM)),
        "w_out": nrm(ks[13], (L, MIX_WIDTH, D_MODEL), MIX_WIDTH ** -0.5),
        "ffn_norm": gain(ks[14], (L, D_MODEL)),
        "peer_query": nrm(ks[15], (L, D_MODEL, PEER_HEADS * PEER_DK), D_MODEL ** -0.5),
        "peer_sub_keys": nrm(ks[16], (L, PEER_HEADS, 2, PEER_NKEYS, PEER_HALF), PEER_HALF ** -0.5),
        "peer_down": nrm(ks[17], (L, PEER_EXPERTS, D_MODEL), D_MODEL ** -0.5),
        "peer_up": nrm(ks[18], (L, PEER_EXPERTS, D_MODEL), (PEER_HEADS * PEER_TOPK) ** -0.5),
    }


def reference(x, attn_norm, w_in, q_norm_moba, k_norm_moba, q_norm_diff, k_norm_diff,
              lambda_q1, lambda_k1, lambda_q2, lambda_k2, moba_out_gain, diff_out_gain,
              w_out, ffn_norm, peer_query, peer_sub_keys, peer_down, peer_up):
    B, S, D = x.shape
    cos, sin = rope_tables(S)
    split_pts = [int(p) for p in np.cumsum(IN_SPLITS)[:-1]]
    for i in range(DEPTH):
        h = rms_norm(x, attn_norm[i])
        proj = h @ w_in[i]
        mq, mk, mv, dq, dk, dv = jnp.split(proj, split_pts, axis=-1)
        mq = mq.reshape(B, S, MOBA_HEADS, HEAD_DIM).transpose(0, 2, 1, 3)
        mk = mk.reshape(B, S, MOBA_HEADS, HEAD_DIM).transpose(0, 2, 1, 3)
        mv = mv.reshape(B, S, MOBA_HEADS, HEAD_DIM).transpose(0, 2, 1, 3)
        mq = apply_rope(rms_norm(mq, q_norm_moba[i]), cos, sin)
        mk = apply_rope(rms_norm(mk, k_norm_moba[i]), cos, sin)
        a_out = moba_attention(mq, mk, mv)
        a_out = rms_norm(a_out.transpose(0, 2, 1, 3), moba_out_gain[i])
        dq = dq.reshape(B, S, DIFF_HEADS, 2, DIFF_QK_DIM).transpose(0, 2, 3, 1, 4)
        dk = dk.reshape(B, S, DIFF_HEADS, 2, DIFF_QK_DIM).transpose(0, 2, 3, 1, 4)
        dv = dv.reshape(B, S, DIFF_HEADS, DIFF_V_DIM).transpose(0, 2, 1, 3)
        dq = apply_rope(rms_norm(dq, q_norm_diff[i][:, None, :]), cos, sin)
        dk = apply_rope(rms_norm(dk, k_norm_diff[i][:, None, :]), cos, sin)
        lambda_init = 0.8 - 0.6 * math.exp(-0.3 * i)
        lam = (jnp.exp(jnp.sum(lambda_q1[i].astype(jnp.float32) * lambda_k1[i].astype(jnp.float32)))
               - jnp.exp(jnp.sum(lambda_q2[i].astype(jnp.float32) * lambda_k2[i].astype(jnp.float32)))
               + lambda_init)
        b_out = diff_attention(dq, dk, dv, lam)
        b_out = rms_norm(b_out.transpose(0, 2, 1, 3), diff_out_gain[i]) * (1.0 - lambda_init)
        mixed = jnp.concatenate([a_out.reshape(B, S, MOBA_WIDTH),
                                 b_out.reshape(B, S, DIFF_WIDTH).astype(a_out.dtype)], axis=-1)
        x = x + mixed @ w_out[i]
        x = x + peer_ffn(rms_norm(x, ffn_norm[i]), peer_query[i], peer_sub_keys[i], peer_down[i], peer_up[i])
    return x
```

```python
import functools
import math

import numpy as np
import jax
import jax.numpy as jnp
from jax import lax
from jax.experimental import pallas as pl
from jax.experimental.pallas import tpu as pltpu

D_MODEL = 1024
HEAD_DIM = 64
HALF_HEAD = HEAD_DIM // 2
MOBA_HEADS = 8
MOBA_WIDTH = MOBA_HEADS * HEAD_DIM
MOBA_BLOCK = 256
MOBA_TOPK = 3
DIFF_HEADS = 4
DIFF_V_DIM = 2 * HEAD_DIM
DIFF_WIDTH = DIFF_HEADS * DIFF_V_DIM
QK_COLS = 2 * MOBA_WIDTH + 2 * DIFF_WIDTH
IN_COLS = QK_COLS + MOBA_WIDTH + DIFF_WIDTH
ROPE_THETA = 10000.0
EPS = 1e-6
NEG_INF = -1e30
LAMBDA_INIT = 0.8 - 0.6 * math.exp(-0.3 * 0)
PEER_HEADS = 8
PEER_NKEYS = 128
PEER_EXPERTS = PEER_NKEYS * PEER_NKEYS
PEER_HALF = 128
PEER_TOPK = 16

LANES = 128
NORM_CHUNK = 256
VMEM_LIMIT = 56 * 1024 * 1024

F32 = jnp.float32
BF16 = jnp.bfloat16


def _nt_dot(a, b):
    return lax.dot_general(a, b, (((1,), (1,)), ((), ())), preferred_element_type=F32)


def _in_proj_kernel(x_ref, g_ref, w_ref, qkg_ref, cos_ref, sin_ref, bd_ref, hm_ref, perm_ref,
                    proj_ref, sel_ref, km_ref, *, nb):
    j = pl.program_id(0) % nb

    xf = x_ref[...]
    h = xf * lax.rsqrt(jnp.mean(xf * xf, axis=-1, keepdims=True) + EPS) * g_ref[...]
    proj = jnp.dot(h.astype(BF16), w_ref[...], preferred_element_type=F32)

    lane = lax.broadcasted_iota(jnp.int32, (MOBA_BLOCK, NORM_CHUNK), 1)
    first_half = (lane % HEAD_DIM) < HALF_HEAD
    cosf = cos_ref[...]
    sinf = sin_ref[...]
    bd = bd_ref[...]
    roped = []
    for c in range(QK_COLS // NORM_CHUNK):
        p = proj[:, c * NORM_CHUNK:(c + 1) * NORM_CHUNK]
        sq = p * p
        hi = sq.astype(BF16)
        lo = (sq - hi.astype(F32)).astype(BF16)
        ms = jnp.dot(hi, bd, preferred_element_type=F32) + jnp.dot(lo, bd, preferred_element_type=F32)
        pn = p * lax.rsqrt(ms + EPS) * qkg_ref[:, c * NORM_CHUNK:(c + 1) * NORM_CHUNK]
        partner = jnp.where(first_half,
                            pltpu.roll(pn, NORM_CHUNK - HALF_HEAD, axis=1),
                            pltpu.roll(pn, HALF_HEAD, axis=1))
        r = pn * cosf + partner * sinf
        roped.append(r)
        proj_ref[:, c * NORM_CHUNK:(c + 1) * NORM_CHUNK] = r.astype(BF16)
    proj_ref[:, QK_COLS:] = proj[:, QK_COLS:].astype(BF16)

    mq = jnp.concatenate(roped[0:2], axis=1)
    mk = jnp.concatenate(roped[2:4], axis=1)

    @pl.when(j == 0)
    def _():
        km_ref[...] = jnp.zeros_like(km_ref)

    km = km_ref[...]
    km_rep = jnp.concatenate(
        [jnp.broadcast_to(km[n:n + 1, :], (MOBA_HEADS, MOBA_WIDTH)) for n in range(nb)], axis=0)
    gate_w = km_rep * hm_ref[...]
    gate_t = lax.dot_general(gate_w, mq, (((1,), (1,)), ((), ())),
                             precision=lax.Precision.HIGHEST, preferred_element_type=F32)
    g = [jnp.where(n < j, gate_t[n * MOBA_HEADS:(n + 1) * MOBA_HEADS, :], NEG_INF) for n in range(nb)]
    sel_rows = []
    for n in range(nb):
        rank = jnp.zeros((MOBA_HEADS, MOBA_BLOCK), F32)
        for m in range(nb):
            if m < n:
                rank = rank + (g[m] >= g[n]).astype(F32)
            elif m > n:
                rank = rank + (g[m] > g[n]).astype(F32)
        sel_rows.append(jnp.where((rank < MOBA_TOPK) & (n < j), 1.0, 0.0))
    pad_rows = LANES - nb * MOBA_HEADS
    sel_t = jnp.concatenate(sel_rows + [jnp.zeros((pad_rows, MOBA_BLOCK), F32)], axis=0)
    sel_tok = sel_t.T.astype(BF16)
    sel_ref[...] = jnp.dot(sel_tok, perm_ref[...], preferred_element_type=F32).astype(BF16)

    km_ref[pl.ds(j, 1), :] = jnp.mean(mk, axis=0, keepdims=True)


def _in_proj(x2d, attn_gain, w_in_p, qk_gain, cosf, sinf, bd, headmask, perm, nb):
    T = x2d.shape[0]
    n_tiles = T // MOBA_BLOCK
    return pl.pallas_call(
        functools.partial(_in_proj_kernel, nb=nb),
        grid=(n_tiles,),
        in_specs=[
            pl.BlockSpec((MOBA_BLOCK, D_MODEL), lambda i: (i, 0)),
            pl.BlockSpec((1, D_MODEL), lambda i: (0, 0)),
            pl.BlockSpec((D_MODEL, IN_COLS), lambda i: (0, 0)),
            pl.BlockSpec((1, QK_COLS), lambda i: (0, 0)),
            pl.BlockSpec((MOBA_BLOCK, NORM_CHUNK), lambda i: (i % nb, 0)),
            pl.BlockSpec((MOBA_BLOCK, NORM_CHUNK), lambda i: (i % nb, 0)),
            pl.BlockSpec((NORM_CHUNK, NORM_CHUNK), lambda i: (0, 0)),
            pl.BlockSpec((nb * MOBA_HEADS, MOBA_WIDTH), lambda i: (0, 0)),
            pl.BlockSpec((LANES, MOBA_WIDTH), lambda i: (0, 0)),
        ],
        out_specs=[
            pl.BlockSpec((MOBA_BLOCK, IN_COLS), lambda i: (i, 0)),
            pl.BlockSpec((MOBA_BLOCK, MOBA_WIDTH), lambda i: (i, 0)),
        ],
        out_shape=[
            jax.ShapeDtypeStruct((T, IN_COLS), BF16),
            jax.ShapeDtypeStruct((T, MOBA_WIDTH), BF16),
        ],
        scratch_shapes=[pltpu.VMEM((nb, MOBA_WIDTH), F32)],
        compiler_params=pltpu.CompilerParams(
            dimension_semantics=("arbitrary",), vmem_limit_bytes=VMEM_LIMIT),
        name="in_proj",
    )(x2d, attn_gain, w_in_p, qk_gain, cosf, sinf, bd, headmask, perm)


def _causal_mask(n):
    row = lax.broadcasted_iota(jnp.int32, (n, n), 0)
    col = lax.broadcasted_iota(jnp.int32, (n, n), 1)
    return col <= row


def _softmax_first(s, v):
    m = jnp.max(s, axis=1, keepdims=True)
    p = jnp.exp(s - m)
    l = jnp.sum(p, axis=1, keepdims=True)
    acc = jnp.dot(p.astype(BF16), v, preferred_element_type=F32)
    return m, l, acc


def _softmax_step(s, v, m_prev, l_prev, acc_prev):
    m_new = jnp.maximum(m_prev, jnp.max(s, axis=1, keepdims=True))
    alpha = jnp.exp(m_prev - m_new)
    p = jnp.exp(s - m_new)
    l = alpha * l_prev + jnp.sum(p, axis=1, keepdims=True)
    acc = alpha * acc_prev + jnp.dot(p.astype(BF16), v, preferred_element_type=F32)
    return m_new, l, acc


def _moba_kernel(q_ref, k_ref, v_ref, sel_ref, gain_ref, o_ref, m_ref, l_ref, acc_ref, *, nb):
    j = pl.program_id(2)
    own = pl.multiple_of(j * MOBA_BLOCK, MOBA_BLOCK)
    causal = _causal_mask(MOBA_BLOCK)
    for hh in range(2):
        cols = slice(hh * HEAD_DIM, (hh + 1) * HEAD_DIM)
        q = q_ref[:, cols]
        s = jnp.where(causal, _nt_dot(q, k_ref[pl.ds(own, MOBA_BLOCK), cols]), NEG_INF)
        m, l, acc = _softmax_first(s, v_ref[pl.ds(own, MOBA_BLOCK), cols])
        m_ref[...] = m
        l_ref[...] = l
        acc_ref[...] = acc
        for n in range(nb - 1):
            @pl.when(n < j)
            def _():
                rows = slice(n * MOBA_BLOCK, (n + 1) * MOBA_BLOCK)
                picked = sel_ref[:, hh * nb + n:hh * nb + n + 1] > 0
                sn = jnp.where(picked, _nt_dot(q, k_ref[rows, cols]), NEG_INF)
                m2, l2, acc2 = _softmax_step(sn, v_ref[rows, cols], m_ref[...], l_ref[...], acc_ref[...])
                m_ref[...] = m2
                l_ref[...] = l2
                acc_ref[...] = acc2
        o = acc_ref[...] / l_ref[...]
        o = o * lax.rsqrt(jnp.mean(o * o, axis=-1, keepdims=True) + EPS) * gain_ref[:, cols]
        o_ref[:, cols] = o.astype(BF16)


def _moba_attn(proj, sel, gain, B, S):
    nb = S // MOBA_BLOCK
    T = B * S
    qcol, kcol, vcol = 0, MOBA_WIDTH // LANES, QK_COLS // LANES
    return pl.pallas_call(
        functools.partial(_moba_kernel, nb=nb),
        grid=(B, MOBA_WIDTH // LANES, nb),
        in_specs=[
            pl.BlockSpec((MOBA_BLOCK, LANES), lambda b, hp, j: (b * nb + j, qcol + hp)),
            pl.BlockSpec((S, LANES), lambda b, hp, j: (b, kcol + hp)),
            pl.BlockSpec((S, LANES), lambda b, hp, j: (b, vcol + hp)),
            pl.BlockSpec((MOBA_BLOCK, LANES), lambda b, hp, j: (b * nb + j, hp)),
            pl.BlockSpec((1, LANES), lambda b, hp, j: (0, hp)),
        ],
        out_specs=pl.BlockSpec((MOBA_BLOCK, LANES), lambda b, hp, j: (b * nb + j, hp)),
        out_shape=jax.ShapeDtypeStruct((T, MOBA_WIDTH), BF16),
        scratch_shapes=[pltpu.VMEM((MOBA_BLOCK, 1), F32), pltpu.VMEM((MOBA_BLOCK, 1), F32),
                        pltpu.VMEM((MOBA_BLOCK, HEAD_DIM), F32)],
        compiler_params=pltpu.CompilerParams(
            dimension_semantics=("arbitrary", "arbitrary", "arbitrary"), vmem_limit_bytes=VMEM_LIMIT),
        name="moba_attn",
    )(proj, proj, proj, sel, gain)


def _diff_kernel(q_ref, k_ref, v_ref, lam_ref, gain_ref, o_ref):
    j = pl.program_id(2)
    own = pl.multiple_of(j * MOBA_BLOCK, MOBA_BLOCK)
    causal = _causal_mask(MOBA_BLOCK)
    outs = []
    for mp in range(2):
        cols = slice(mp * HEAD_DIM, (mp + 1) * HEAD_DIM)
        q = q_ref[:, cols]
        s = jnp.where(causal, _nt_dot(q, k_ref[pl.ds(own, MOBA_BLOCK), cols]), NEG_INF)
        carry = _softmax_first(s, v_ref[pl.ds(own, MOBA_BLOCK), :])

        def body(n, carry, q=q, cols=cols):
            rows = pl.ds(pl.multiple_of(n * MOBA_BLOCK, MOBA_BLOCK), MOBA_BLOCK)
            return _softmax_step(_nt_dot(q, k_ref[rows, cols]), v_ref[rows, :], *carry)

        m, l, acc = lax.fori_loop(0, j, body, carry)
        outs.append(acc / l)
    lp = lam_ref[...]
    lam = (jnp.exp(jnp.sum(lp[0:1] * lp[1:2], axis=1, keepdims=True))
           - jnp.exp(jnp.sum(lp[2:3] * lp[3:4], axis=1, keepdims=True)) + LAMBDA_INIT)
    o = outs[0] - lam * outs[1]
    o = o * lax.rsqrt(jnp.mean(o * o, axis=-1, keepdims=True) + EPS) * gain_ref[...] * (1.0 - LAMBDA_INIT)
    o_ref[...] = o.astype(BF16)


def _diff_attn(proj, lam_params, gain, B, S):
    nb = S // MOBA_BLOCK
    T = B * S
    qcol = 2 * MOBA_WIDTH // LANES
    kcol = qcol + DIFF_WIDTH // LANES
    vcol = (QK_COLS + MOBA_WIDTH) // LANES
    return pl.pallas_call(
        _diff_kernel,
        grid=(B, DIFF_HEADS, nb),
        in_specs=[
            pl.BlockSpec((MOBA_BLOCK, LANES), lambda b, h, j: (b * nb + j, qcol + h)),
            pl.BlockSpec((S, LANES), lambda b, h, j: (b, kcol + h)),
            pl.BlockSpec((S, LANES), lambda b, h, j: (b, vcol + h)),
            pl.BlockSpec((4, HEAD_DIM), lambda b, h, j: (0, 0)),
            pl.BlockSpec((1, LANES), lambda b, h, j: (0, h)),
        ],
        out_specs=pl.BlockSpec((MOBA_BLOCK, LANES), lambda b, h, j: (b * nb + j, h)),
        out_shape=jax.ShapeDtypeStruct((T, DIFF_WIDTH), BF16),
        compiler_params=pltpu.CompilerParams(
            dimension_semantics=("arbitrary", "arbitrary", "arbitrary"), vmem_limit_bytes=VMEM_LIMIT),
        name="diff_attn",
    )(proj, proj, proj, lam_params, gain)


OUT_TM = 512


def _out_proj_kernel(x_ref, a_ref, b_ref, wa_ref, wb_ref, g_ref, x2_ref, xn_ref):
    x2 = (x_ref[...] + jnp.dot(a_ref[...], wa_ref[...], preferred_element_type=F32)
          + jnp.dot(b_ref[...], wb_ref[...], preferred_element_type=F32))
    x2_ref[...] = x2
    xn = x2 * lax.rsqrt(jnp.mean(x2 * x2, axis=-1, keepdims=True) + EPS) * g_ref[...]
    xn_ref[...] = xn.astype(BF16)


def _out_proj(x2d, a_out, b_out, w_a, w_b, ffn_gain):
    T = x2d.shape[0]
    return pl.pallas_call(
        _out_proj_kernel,
        grid=(T // OUT_TM,),
        in_specs=[
            pl.BlockSpec((OUT_TM, D_MODEL), lambda i: (i, 0)),
            pl.BlockSpec((OUT_TM, MOBA_WIDTH), lambda i: (i, 0)),
            pl.BlockSpec((OUT_TM, DIFF_WIDTH), lambda i: (i, 0)),
            pl.BlockSpec((MOBA_WIDTH, D_MODEL), lambda i: (0, 0)),
            pl.BlockSpec((DIFF_WIDTH, D_MODEL), lambda i: (0, 0)),
            pl.BlockSpec((1, D_MODEL), lambda i: (0, 0)),
        ],
        out_specs=[
            pl.BlockSpec((OUT_TM, D_MODEL), lambda i: (i, 0)),
            pl.BlockSpec((OUT_TM, D_MODEL), lambda i: (i, 0)),
        ],
        out_shape=[
            jax.ShapeDtypeStruct((T, D_MODEL), F32),
            jax.ShapeDtypeStruct((T, D_MODEL), BF16),
        ],
        compiler_params=pltpu.CompilerParams(
            dimension_semantics=("arbitrary",), vmem_limit_bytes=VMEM_LIMIT),
        name="out_proj",
    )(x2d, a_out, b_out, w_a, w_b, ffn_gain)


PEER_TM = 512
PEER_ROWS = 8
PEER_TE = PEER_ROWS * PEER_NKEYS
N_TOP = PEER_TOPK + 1
CAND_PAIRS = [(a, b) for a in range(N_TOP) for b in range(N_TOP) if (a + 1) * (b + 1) <= N_TOP]


def _top_rows(s, k):
    out = []
    for _ in range(k):
        m = jnp.max(s, axis=0, keepdims=True)
        out.append(m)
        s = jnp.where(s >= m, -jnp.inf, s)
    return out


def _peer_kernel(xn_ref, x2_ref, wq_ref, sk_ref, u_ref, vt_ref, o_ref,
                 s2_ref, e2_ref, thr_ref, e1_ref, acc_ref, wt_ref):
    e = pl.program_id(1)
    xn = xn_ref[...]

    @pl.when(e == 0)
    def _():
        q_t = _nt_dot(wq_ref[...], xn).astype(BF16)
        for h in range(PEER_HEADS):
            s = [jnp.dot(sk_ref[2 * h + p], q_t[(2 * h + p) * PEER_HALF:(2 * h + p + 1) * PEER_HALF, :],
                         preferred_element_type=F32) for p in range(2)]
            top = [_top_rows(s[p], N_TOP) for p in range(2)]
            cands = [top[0][a] + top[1][b] for (a, b) in CAND_PAIRS]
            pad = (-len(cands)) % 8
            c_all = jnp.concatenate(cands + [jnp.full_like(cands[0], -jnp.inf)] * pad, axis=0)
            best = _top_rows(c_all, N_TOP)
            thr = 0.5 * (best[PEER_TOPK - 1] + best[PEER_TOPK])
            z = jnp.ones_like(best[0])
            for r in range(1, PEER_TOPK):
                z = z + jnp.exp(best[r] - best[0])
            s2_ref[h] = s[1]
            e2_ref[h] = jnp.exp(s[1] - top[1][0]) / z
            e1_ref[h] = jnp.exp(s[0] - top[0][0])
            thr_ref[h] = thr - s[0]

    a_t = _nt_dot(u_ref[...], xn)
    for r in range(PEER_ROWS):
        i1 = e * PEER_ROWS + r
        g = jnp.zeros((PEER_NKEYS, PEER_TM), F32)
        for h in range(PEER_HEADS):
            thr_row = thr_ref[h, pl.ds(i1, 1), :]
            e1_row = e1_ref[h, pl.ds(i1, 1), :]
            g = g + jnp.where(s2_ref[h] >= thr_row, e2_ref[h] * e1_row, 0.0)
        a = a_t[r * PEER_NKEYS:(r + 1) * PEER_NKEYS, :]
        act = 0.5 * a * (1.0 + lax.erf(a * math.sqrt(0.5)))
        wt_ref[r * PEER_NKEYS:(r + 1) * PEER_NKEYS, :] = (g * act).astype(BF16)
    part = jnp.dot(vt_ref[...], wt_ref[...], preferred_element_type=F32)

    @pl.when(e == 0)
    def _():
        acc_ref[...] = part

    @pl.when(e > 0)
    def _():
        acc_ref[...] += part

    @pl.when(e == pl.num_programs(1) - 1)
    def _():
        o_ref[...] = x2_ref[...] + acc_ref[...].T


def _peer(xn, x2, wq_t, sub_keys, u, v_t):
    T = xn.shape[0]
    head_buf = pltpu.VMEM((PEER_HEADS, PEER_NKEYS, PEER_TM), F32)
    return pl.pallas_call(
        _peer_kernel,
        grid=(T // PEER_TM, PEER_EXPERTS // PEER_TE),
        in_specs=[
            pl.BlockSpec((PEER_TM, D_MODEL), lambda i, e: (i, 0)),
            pl.BlockSpec((PEER_TM, D_MODEL), lambda i, e: (i, 0)),
            pl.BlockSpec((2 * PEER_HEADS * PEER_HALF, D_MODEL), lambda i, e: (0, 0)),
            pl.BlockSpec((2 * PEER_HEADS, PEER_NKEYS, PEER_HALF), lambda i, e: (0, 0, 0)),
            pl.BlockSpec((PEER_TE, D_MODEL), lambda i, e: (e, 0)),
            pl.BlockSpec((D_MODEL, PEER_TE), lambda i, e: (0, e)),
        ],
        out_specs=pl.BlockSpec((PEER_TM, D_MODEL), lambda i, e: (i, 0)),
        out_shape=jax.ShapeDtypeStruct((T, D_MODEL), F32),
        scratch_shapes=[head_buf, head_buf, head_buf, head_buf,
                        pltpu.VMEM((D_MODEL, PEER_TM), F32),
                        pltpu.VMEM((PEER_TE, PEER_TM), BF16)],
        compiler_params=pltpu.CompilerParams(
            dimension_semantics=("arbitrary", "arbitrary"), vmem_limit_bytes=VMEM_LIMIT),
        name="peer",
    )(xn, x2, wq_t, sub_keys, u, v_t)


def _rope_tables(S):
    pos = jnp.arange(S, dtype=F32)
    inv = 1.0 / (ROPE_THETA ** (jnp.arange(0, HEAD_DIM, 2, dtype=F32) / HEAD_DIM))
    ang = pos[:, None] * inv[None, :]
    cos, sin = jnp.cos(ang), jnp.sin(ang)
    reps = NORM_CHUNK // HEAD_DIM
    cosf = jnp.tile(jnp.concatenate([cos, cos], axis=1), (1, reps))
    sinf = jnp.tile(jnp.concatenate([-sin, sin], axis=1), (1, reps))
    return cosf, sinf


def _constants(nb):
    idx = np.arange(NORM_CHUNK)
    bd = (idx[:, None] // HEAD_DIM == idx[None, :] // HEAD_DIM).astype(np.float32) / HEAD_DIM
    rows = np.arange(nb * MOBA_HEADS)
    cols = np.arange(MOBA_WIDTH)
    headmask = (cols[None, :] // HEAD_DIM == rows[:, None] % MOBA_HEADS).astype(np.float32)
    perm = np.zeros((LANES, MOBA_WIDTH), np.float32)
    for n in range(nb):
        for h in range(MOBA_HEADS):
            perm[n * MOBA_HEADS + h, (h // 2) * LANES + (h % 2) * nb + n] = 1.0
    return jnp.asarray(bd, BF16), jnp.asarray(headmask), jnp.asarray(perm, BF16)


def kernel(x, attn_norm, w_in, q_norm_moba, k_norm_moba, q_norm_diff, k_norm_diff, lambda_q1, lambda_k1, lambda_q2, lambda_k2, moba_out_gain, diff_out_gain, w_out, ffn_norm, peer_query, peer_sub_keys, peer_down, peer_up):
    B, S, D = x.shape
    assert D == D_MODEL and S % MOBA_BLOCK == 0 and (B * S) % PEER_TM == 0
    nb = S // MOBA_BLOCK
    assert 2 * nb <= LANES and nb * MOBA_HEADS <= LANES
    T = B * S
    x2d = x.reshape(T, D)
    cosf, sinf = _rope_tables(S)
    bd, headmask, perm = _constants(nb)

    i = 0
    w = w_in[i]
    mqw, mkw, mvw, dqw, dkw, dvw = jnp.split(w, [512, 1024, 1536, 2048, 2560], axis=1)
    w_in_p = jnp.concatenate([mqw, mkw, dqw, dkw, mvw, dvw], axis=1).astype(BF16)
    scale = HEAD_DIM ** -0.5
    qk_gain = jnp.concatenate([
        jnp.tile(q_norm_moba[i], MOBA_HEADS) * scale,
        jnp.tile(k_norm_moba[i], MOBA_HEADS),
        jnp.tile(q_norm_diff[i].reshape(-1), DIFF_HEADS) * scale,
        jnp.tile(k_norm_diff[i].reshape(-1), DIFF_HEADS),
    ]).reshape(1, QK_COLS).astype(F32)

    proj, sel = _in_proj(x2d, attn_norm[i].reshape(1, D), w_in_p, qk_gain, cosf, sinf,
                         bd, headmask, perm, nb)
    a_out = _moba_attn(proj, sel, moba_out_gain[i].reshape(1, MOBA_WIDTH), B, S)
    lam_params = jnp.stack([lambda_q1[i], lambda_k1[i], lambda_q2[i], lambda_k2[i]]).astype(F32)
    b_out = _diff_attn(proj, lam_params, diff_out_gain[i].reshape(1, DIFF_WIDTH), B, S)

    w_o = w_out[i].astype(BF16)
    x2, xn = _out_proj(x2d, a_out, b_out, w_o[:MOBA_WIDTH], w_o[MOBA_WIDTH:], ffn_norm[i].reshape(1, D))

    wq_t = peer_query[i].T.astype(BF16)
    sub_keys = peer_sub_keys[i].reshape(2 * PEER_HEADS, PEER_NKEYS, PEER_HALF).astype(BF16)
    u = peer_down[i].astype(BF16)
    v_t = peer_up[i].T.astype(BF16)
    out = _peer(xn, x2, wq_t, sub_keys, u, v_t)
    return out.reshape(B, S, D)
```

```python
import functools
import math

import numpy as np
import jax
import jax.numpy as jnp
from jax import lax
from jax.experimental import pallas as pl
from jax.experimental.pallas import tpu as pltpu

D_MODEL = 1024
HEAD_DIM = 64
HALF_HEAD = HEAD_DIM // 2
MOBA_HEADS = 8
MOBA_WIDTH = MOBA_HEADS * HEAD_DIM
MOBA_BLOCK = 256
MOBA_TOPK = 3
DIFF_HEADS = 4
DIFF_V_DIM = 2 * HEAD_DIM
DIFF_WIDTH = DIFF_HEADS * DIFF_V_DIM
QK_COLS = 2 * MOBA_WIDTH + 2 * DIFF_WIDTH
V_COLS = MOBA_WIDTH + DIFF_WIDTH
N_CHAINS = 8
ROPE_THETA = 10000.0
EPS = 1e-6
NEG_INF = -1e30
LAMBDA_INIT = 0.8 - 0.6 * math.exp(-0.3 * 0)
PEER_HEADS = 8
PEER_NKEYS = 128
PEER_EXPERTS = PEER_NKEYS * PEER_NKEYS
PEER_HALF = 128
PEER_TOPK = 16

LANES = 128
BF16_ROWS = 16
MXU_DIM = 256
VMEM_LIMIT = 56 * 1024 * 1024

F32 = jnp.float32
BF16 = jnp.bfloat16


def _nt_dot(a, b):
    return lax.dot_general(a, b, (((1,), (1,)), ((), ())), preferred_element_type=F32)


def _in_proj_kernel(x_ref, g_ref, w_ref, wvt_ref, qkg_ref, cos_ref, sin_ref, bd_ref, hm_ref,
                    proj_ref, vt_ref, sel_ref, km_ref, *, nb):
    j = pl.program_id(0) % nb

    xf = x_ref[...]
    h = (xf * lax.rsqrt(jnp.mean(xf * xf, axis=-1, keepdims=True) + EPS) * g_ref[...]).astype(BF16)
    proj = jnp.dot(h, w_ref[...], preferred_element_type=F32)
    vt_ref[0] = _nt_dot(wvt_ref[...], h).astype(BF16)

    lane = lax.broadcasted_iota(jnp.int32, (MOBA_BLOCK, MXU_DIM), 1)
    first_half = (lane % HEAD_DIM) < HALF_HEAD
    cosf = cos_ref[...]
    sinf = sin_ref[...]
    bd = bd_ref[...]
    roped = []
    for c in range(QK_COLS // MXU_DIM):
        p = proj[:, c * MXU_DIM:(c + 1) * MXU_DIM]
        sq = p * p
        hi = sq.astype(BF16)
        lo = (sq - hi.astype(F32)).astype(BF16)
        ms = jnp.dot(hi, bd, preferred_element_type=F32) + jnp.dot(lo, bd, preferred_element_type=F32)
        pn = p * lax.rsqrt(ms + EPS) * qkg_ref[:, c * MXU_DIM:(c + 1) * MXU_DIM]
        partner = jnp.where(first_half,
                            pltpu.roll(pn, MXU_DIM - HALF_HEAD, axis=1),
                            pltpu.roll(pn, HALF_HEAD, axis=1))
        r = pn * cosf + partner * sinf
        roped.append(r)
        proj_ref[:, c * MXU_DIM:(c + 1) * MXU_DIM] = r.astype(BF16)

    mq = jnp.concatenate(roped[0:2], axis=1)
    mk = jnp.concatenate(roped[2:4], axis=1)

    @pl.when(j == 0)
    def _():
        km_ref[...] = jnp.zeros_like(km_ref)

    km = km_ref[...]
    km_rep = jnp.concatenate(
        [jnp.broadcast_to(km[n:n + 1, :], (MOBA_HEADS, MOBA_WIDTH)) for n in range(nb)], axis=0)
    gate_w = km_rep * hm_ref[...]
    gate_t = lax.dot_general(gate_w, mq, (((1,), (1,)), ((), ())),
                             precision=lax.Precision.HIGHEST, preferred_element_type=F32)
    g = [jnp.where(n < j, gate_t[n * MOBA_HEADS:(n + 1) * MOBA_HEADS, :], NEG_INF) for n in range(nb)]
    sel_rows = []
    for n in range(nb):
        rank = jnp.zeros((MOBA_HEADS, MOBA_BLOCK), F32)
        for m in range(nb):
            if m < n:
                rank = rank + (g[m] >= g[n]).astype(F32)
            elif m > n:
                rank = rank + (g[m] > g[n]).astype(F32)
        sel_rows.append(jnp.where((rank < MOBA_TOPK) & (n < j), 1.0, 0.0))
    sel_ref[...] = jnp.concatenate(sel_rows, axis=0)

    km_ref[pl.ds(j, 1), :] = jnp.mean(mk, axis=0, keepdims=True)


def _in_proj(x2d, attn_gain, w_qk, w_v_t, qk_gain, cosf, sinf, bd, headmask, nb):
    T = x2d.shape[0]
    n_tiles = T // MOBA_BLOCK
    sel_rows = nb * MOBA_HEADS
    return pl.pallas_call(
        functools.partial(_in_proj_kernel, nb=nb),
        grid=(n_tiles,),
        in_specs=[
            pl.BlockSpec((MOBA_BLOCK, D_MODEL), lambda i: (i, 0)),
            pl.BlockSpec((1, D_MODEL), lambda i: (0, 0)),
            pl.BlockSpec((D_MODEL, QK_COLS), lambda i: (0, 0)),
            pl.BlockSpec((V_COLS, D_MODEL), lambda i: (0, 0)),
            pl.BlockSpec((1, QK_COLS), lambda i: (0, 0)),
            pl.BlockSpec((MOBA_BLOCK, MXU_DIM), lambda i: (i % nb, 0)),
            pl.BlockSpec((MOBA_BLOCK, MXU_DIM), lambda i: (i % nb, 0)),
            pl.BlockSpec((MXU_DIM, MXU_DIM), lambda i: (0, 0)),
            pl.BlockSpec((sel_rows, MOBA_WIDTH), lambda i: (0, 0)),
        ],
        out_specs=[
            pl.BlockSpec((MOBA_BLOCK, QK_COLS), lambda i: (i, 0)),
            pl.BlockSpec((1, V_COLS, MOBA_BLOCK), lambda i: (i, 0, 0)),
            pl.BlockSpec((sel_rows, MOBA_BLOCK), lambda i: (i, 0)),
        ],
        out_shape=[
            jax.ShapeDtypeStruct((T, QK_COLS), BF16),
            jax.ShapeDtypeStruct((n_tiles, V_COLS, MOBA_BLOCK), BF16),
            jax.ShapeDtypeStruct((n_tiles * sel_rows, MOBA_BLOCK), F32),
        ],
        scratch_shapes=[pltpu.VMEM((nb, MOBA_WIDTH), F32)],
        compiler_params=pltpu.CompilerParams(
            dimension_semantics=("arbitrary",), vmem_limit_bytes=VMEM_LIMIT),
        name="in_proj",
    )(x2d, attn_gain, w_qk, w_v_t, qk_gain, cosf, sinf, bd, headmask)


def _causal_mask_t(n):
    key = lax.broadcasted_iota(jnp.int32, (n, n), 0)
    qry = lax.broadcasted_iota(jnp.int32, (n, n), 1)
    return key <= qry


def _attn_region(first, k_blk, v_blk, mask_fn, q_ref, refs, v_rows, v_slice):
    s_ref, p_ref, m_ref, l_ref, alpha_ref, acc_ref = refs
    for c in range(N_CHAINS):
        cols = slice(c * HEAD_DIM, (c + 1) * HEAD_DIM)
        s = mask_fn(c, _nt_dot(k_blk[:, cols], q_ref[:, cols]))
        s_ref[c] = s
        mx = jnp.max(s, axis=0, keepdims=True)
        if first:
            m_ref[c:c + 1, :] = mx
        else:
            m_prev = m_ref[c:c + 1, :]
            m_new = jnp.maximum(m_prev, mx)
            alpha_ref[c:c + 1, :] = jnp.exp(m_prev - m_new)
            m_ref[c:c + 1, :] = m_new
    for c in range(N_CHAINS):
        p = jnp.exp(s_ref[c] - m_ref[c:c + 1, :])
        lsum = jnp.sum(p, axis=0, keepdims=True)
        if first:
            l_ref[c:c + 1, :] = lsum
        else:
            l_ref[c:c + 1, :] = alpha_ref[c:c + 1, :] * l_ref[c:c + 1, :] + lsum
        p_ref[c] = p.astype(BF16)
    for c in range(N_CHAINS):
        rows = slice(c * v_rows, (c + 1) * v_rows)
        pv = jnp.dot(v_blk[v_slice(c), :], p_ref[c], preferred_element_type=F32)
        if first:
            acc_ref[rows, :] = pv
        else:
            acc_ref[rows, :] = alpha_ref[c:c + 1, :] * acc_ref[rows, :] + pv


def _attn_kernel(q_ref, k_ref, vt_ref, *rest, nb, moba):
    if moba:
        sel_ref, gain_ref, o_ref, s_ref, p_ref, m_ref, l_ref, alpha_ref, acc_ref = rest
        v_rows, chains_per_v = HEAD_DIM, 1
    else:
        lam_ref, gain_ref, o_ref, s_ref, p_ref, m_ref, l_ref, alpha_ref, acc_ref = rest
        v_rows, chains_per_v = DIFF_V_DIM, 2
    refs = (s_ref, p_ref, m_ref, l_ref, alpha_ref, acc_ref)
    j = pl.program_id(1)
    own = pl.multiple_of(j * MOBA_BLOCK, MOBA_BLOCK)
    causal = _causal_mask_t(MOBA_BLOCK)

    def v_slice(c):
        hv = c // chains_per_v
        return slice(hv * v_rows, (hv + 1) * v_rows)

    _attn_region(True, k_ref[pl.ds(own, MOBA_BLOCK), :], vt_ref[j],
                 lambda c, s: jnp.where(causal, s, NEG_INF), q_ref, refs, v_rows, v_slice)

    for n in range(nb - 1):
        @pl.when(n < j)
        def _():
            def mask_fn(c, s):
                if not moba:
                    return s
                picked = sel_ref[n * MOBA_HEADS + c:n * MOBA_HEADS + c + 1, :] > 0.5
                return jnp.where(picked, s, NEG_INF)

            _attn_region(False, k_ref[n * MOBA_BLOCK:(n + 1) * MOBA_BLOCK, :], vt_ref[n],
                         mask_fn, q_ref, refs, v_rows, v_slice)

    if moba:
        for hp in range(MOBA_HEADS // 2):
            halves = []
            for c in (2 * hp, 2 * hp + 1):
                o = acc_ref[c * v_rows:(c + 1) * v_rows, :] / l_ref[c:c + 1, :]
                halves.append(o * lax.rsqrt(jnp.mean(o * o, axis=0, keepdims=True) + EPS))
            pair = jnp.concatenate(halves, axis=0)
            o_ref[:, hp * LANES:(hp + 1) * LANES] = (pair.T * gain_ref[:, hp * LANES:(hp + 1) * LANES]).astype(BF16)
    else:
        lp = lam_ref[...]
        lam = (jnp.exp(jnp.sum(lp[0:1] * lp[1:2], axis=1, keepdims=True))
               - jnp.exp(jnp.sum(lp[2:3] * lp[3:4], axis=1, keepdims=True)) + LAMBDA_INIT)
        for hd in range(DIFF_HEADS):
            o1 = acc_ref[(2 * hd) * v_rows:(2 * hd + 1) * v_rows, :] / l_ref[2 * hd:2 * hd + 1, :]
            o2 = acc_ref[(2 * hd + 1) * v_rows:(2 * hd + 2) * v_rows, :] / l_ref[2 * hd + 1:2 * hd + 2, :]
            o = o1 - lam * o2
            o = o * lax.rsqrt(jnp.mean(o * o, axis=0, keepdims=True) + EPS)
            gain = gain_ref[:, hd * LANES:(hd + 1) * LANES] * (1.0 - LAMBDA_INIT)
            o_ref[:, hd * LANES:(hd + 1) * LANES] = (o.T * gain).astype(BF16)


def _attention(proj, v_t, extra, gain, B, S, *, moba):
    nb = S // MOBA_BLOCK
    T = B * S
    width = N_CHAINS * HEAD_DIM
    qcol = 0 if moba else 2
    v_blk = 0 if moba else 1
    v_rows = HEAD_DIM if moba else DIFF_V_DIM
    if moba:
        extra_spec = pl.BlockSpec((nb * MOBA_HEADS, MOBA_BLOCK), lambda b, j: (b * nb + j, 0))
    else:
        extra_spec = pl.BlockSpec((4, HEAD_DIM), lambda b, j: (0, 0))
    return pl.pallas_call(
        functools.partial(_attn_kernel, nb=nb, moba=moba),
        grid=(B, nb),
        in_specs=[
            pl.BlockSpec((MOBA_BLOCK, width), lambda b, j: (b * nb + j, qcol)),
            pl.BlockSpec((S, width), lambda b, j: (b, qcol + 1)),
            pl.BlockSpec((nb, width, MOBA_BLOCK), lambda b, j: (b, v_blk, 0)),
            extra_spec,
            pl.BlockSpec((1, width), lambda b, j: (0, 0)),
        ],
        out_specs=pl.BlockSpec((MOBA_BLOCK, width), lambda b, j: (b * nb + j, 0)),
        out_shape=jax.ShapeDtypeStruct((T, width), BF16),
        scratch_shapes=[pltpu.VMEM((N_CHAINS, MOBA_BLOCK, MOBA_BLOCK), F32),
                        pltpu.VMEM((N_CHAINS, MOBA_BLOCK, MOBA_BLOCK), BF16),
                        pltpu.VMEM((N_CHAINS, MOBA_BLOCK), F32), pltpu.VMEM((N_CHAINS, MOBA_BLOCK), F32),
                        pltpu.VMEM((N_CHAINS, MOBA_BLOCK), F32),
                        pltpu.VMEM((N_CHAINS * v_rows, MOBA_BLOCK), F32)],
        compiler_params=pltpu.CompilerParams(
            dimension_semantics=("arbitrary", "arbitrary"), vmem_limit_bytes=VMEM_LIMIT),
        name="moba_attn" if moba else "diff_attn",
    )(proj, proj, v_t, extra, gain)


OUT_TM = 512


def _out_proj_kernel(x_ref, a_ref, b_ref, wa_ref, wb_ref, g_ref, x2_ref, xn_ref):
    x2 = (x_ref[...] + jnp.dot(a_ref[...], wa_ref[...], preferred_element_type=F32)
          + jnp.dot(b_ref[...], wb_ref[...], preferred_element_type=F32))
    x2_ref[...] = x2
    xn = x2 * lax.rsqrt(jnp.mean(x2 * x2, axis=-1, keepdims=True) + EPS) * g_ref[...]
    xn_ref[...] = xn.astype(BF16)


def _out_proj(x2d, a_out, b_out, w_a, w_b, ffn_gain):
    T = x2d.shape[0]
    return pl.pallas_call(
        _out_proj_kernel,
        grid=(T // OUT_TM,),
        in_specs=[
            pl.BlockSpec((OUT_TM, D_MODEL), lambda i: (i, 0)),
            pl.BlockSpec((OUT_TM, MOBA_WIDTH), lambda i: (i, 0)),
            pl.BlockSpec((OUT_TM, DIFF_WIDTH), lambda i: (i, 0)),
            pl.BlockSpec((MOBA_WIDTH, D_MODEL), lambda i: (0, 0)),
            pl.BlockSpec((DIFF_WIDTH, D_MODEL), lambda i: (0, 0)),
            pl.BlockSpec((1, D_MODEL), lambda i: (0, 0)),
        ],
        out_specs=[
            pl.BlockSpec((OUT_TM, D_MODEL), lambda i: (i, 0)),
            pl.BlockSpec((OUT_TM, D_MODEL), lambda i: (i, 0)),
        ],
        out_shape=[
            jax.ShapeDtypeStruct((T, D_MODEL), F32),
            jax.ShapeDtypeStruct((T, D_MODEL), BF16),
        ],
        compiler_params=pltpu.CompilerParams(
            dimension_semantics=("arbitrary",), vmem_limit_bytes=VMEM_LIMIT),
        name="out_proj",
    )(x2d, a_out, b_out, w_a, w_b, ffn_gain)


PEER_TM = 512
PEER_ROWS = 16
PEER_TE = PEER_ROWS * PEER_NKEYS
ROWS_PER_DOT = MXU_DIM // PEER_NKEYS
N_TOP = PEER_TOPK + 1
CAND_PAIRS = [(a, b) for a in range(N_TOP) for b in range(N_TOP) if (a + 1) * (b + 1) <= N_TOP]
KEY_PIECES = PEER_NKEYS // BF16_ROWS


def _top_rows(s, k, want_rank=False):
    out = []
    rank = jnp.full(s.shape, float(k), F32) if want_rank else None
    for r in range(k):
        m = jnp.max(s, axis=0, keepdims=True)
        out.append(m)
        hit = s >= m
        if want_rank:
            rank = jnp.where(hit, float(r), rank)
        s = jnp.where(hit, -jnp.inf, s)
    return out, rank


def _peer_kernel(xn_ref, x2_ref, wq_ref, sk_ref, u_ref, vt_ref, o_ref,
                 r2_ref, e2_ref, c_ref, e1_ref, acc_ref, wt_ref):
    e = pl.program_id(1)
    xn = xn_ref[...]

    @pl.when(e == 0)
    def _():
        q_t = _nt_dot(wq_ref[...], xn).astype(BF16)
        for h in range(PEER_HEADS):
            s = [jnp.dot(sk_ref[2 * h + p], q_t[(2 * h + p) * PEER_HALF:(2 * h + p + 1) * PEER_HALF, :],
                         preferred_element_type=F32) for p in range(2)]
            top1, _ = _top_rows(s[0], N_TOP)
            top2, rank2 = _top_rows(s[1], N_TOP, want_rank=True)
            cands = [top1[a] + top2[b] for (a, b) in CAND_PAIRS]
            pad = (-len(cands)) % 8
            c_all = jnp.concatenate(cands + [jnp.full_like(cands[0], -jnp.inf)] * pad, axis=0)
            best, _ = _top_rows(c_all, N_TOP)
            thr = 0.5 * (best[PEER_TOPK - 1] + best[PEER_TOPK])
            z = jnp.ones_like(best[0])
            for r in range(1, PEER_TOPK):
                z = z + jnp.exp(best[r] - best[0])
            count = jnp.zeros_like(s[0])
            for b in range(PEER_TOPK):
                count = count + jnp.where(s[0] >= thr - top2[b], 1.0, 0.0)
            r2_ref[h] = rank2.astype(BF16)
            e2_ref[h] = (jnp.exp(s[1] - top2[0]) / z).astype(BF16)
            e1_ref[h] = jnp.exp(s[0] - top1[0])
            c_ref[h] = count

    def scores(cb):
        return _nt_dot(u_ref[cb * MXU_DIM:(cb + 1) * MXU_DIM, :], xn)

    def gate_weights(cb, a_t):
        for rr in range(ROWS_PER_DOT):
            i1 = e * PEER_ROWS + cb * ROWS_PER_DOT + rr
            g = [jnp.zeros((BF16_ROWS, PEER_TM), BF16) for _ in range(KEY_PIECES)]
            for h in range(PEER_HEADS):
                c_b = jnp.broadcast_to(c_ref[h, pl.ds(i1, 1), :], (BF16_ROWS, PEER_TM)).astype(BF16)
                e1_b = jnp.broadcast_to(e1_ref[h, pl.ds(i1, 1), :], (BF16_ROWS, PEER_TM)).astype(BF16)
                for k in range(KEY_PIECES):
                    piece = slice(k * BF16_ROWS, (k + 1) * BF16_ROWS)
                    keep = r2_ref[h, piece, :] < c_b
                    g[k] = g[k] + jnp.where(keep, e2_ref[h, piece, :] * e1_b, jnp.zeros_like(e1_b))
            a = a_t[rr * PEER_NKEYS:(rr + 1) * PEER_NKEYS, :]
            act = (a * (0.5 + 0.5 * lax.erf(a * math.sqrt(0.5)))).astype(BF16)
            base = (cb * ROWS_PER_DOT + rr) * PEER_NKEYS
            for k in range(KEY_PIECES):
                wt_ref[base + k * BF16_ROWS:base + (k + 1) * BF16_ROWS, :] = (
                    g[k] * act[k * BF16_ROWS:(k + 1) * BF16_ROWS, :])

    def up_project(cb):
        return jnp.dot(vt_ref[:, cb * MXU_DIM:(cb + 1) * MXU_DIM],
                       wt_ref[cb * MXU_DIM:(cb + 1) * MXU_DIM, :], preferred_element_type=F32)

    n_chunks = PEER_ROWS // ROWS_PER_DOT
    a_next = scores(0)
    part = None
    for cb in range(n_chunks):
        a_cur = a_next
        if cb + 1 < n_chunks:
            a_next = scores(cb + 1)
        gate_weights(cb, a_cur)
        up = up_project(cb)
        part = up if part is None else part + up

    @pl.when(e == 0)
    def _():
        acc_ref[...] = jnp.zeros_like(acc_ref)

    acc_ref[...] += part

    @pl.when(e == pl.num_programs(1) - 1)
    def _():
        o_ref[...] = x2_ref[...] + acc_ref[...].T


def _peer(xn, x2, wq_t, sub_keys, u, v_t):
    T = xn.shape[0]
    head_f32 = pltpu.VMEM((PEER_HEADS, PEER_NKEYS, PEER_TM), F32)
    head_bf16 = pltpu.VMEM((PEER_HEADS, PEER_NKEYS, PEER_TM), BF16)
    return pl.pallas_call(
        _peer_kernel,
        grid=(T // PEER_TM, PEER_EXPERTS // PEER_TE),
        in_specs=[
            pl.BlockSpec((PEER_TM, D_MODEL), lambda i, e: (i, 0)),
            pl.BlockSpec((PEER_TM, D_MODEL), lambda i, e: (i, 0)),
            pl.BlockSpec((2 * PEER_HEADS * PEER_HALF, D_MODEL), lambda i, e: (0, 0)),
            pl.BlockSpec((2 * PEER_HEADS, PEER_NKEYS, PEER_HALF), lambda i, e: (0, 0, 0)),
            pl.BlockSpec((PEER_TE, D_MODEL), lambda i, e: (e, 0)),
            pl.BlockSpec((D_MODEL, PEER_TE), lambda i, e: (0, e)),
        ],
        out_specs=pl.BlockSpec((PEER_TM, D_MODEL), lambda i, e: (i, 0)),
        out_shape=jax.ShapeDtypeStruct((T, D_MODEL), F32),
        scratch_shapes=[head_bf16, head_bf16, head_f32, head_f32,
                        pltpu.VMEM((D_MODEL, PEER_TM), F32),
                        pltpu.VMEM((PEER_TE, PEER_TM), BF16)],
        compiler_params=pltpu.CompilerParams(
            dimension_semantics=("arbitrary", "arbitrary"), vmem_limit_bytes=VMEM_LIMIT),
        name="peer",
    )(xn, x2, wq_t, sub_keys, u, v_t)


def _rope_tables(S):
    pos = jnp.arange(S, dtype=F32)
    inv = 1.0 / (ROPE_THETA ** (jnp.arange(0, HEAD_DIM, 2, dtype=F32) / HEAD_DIM))
    ang = pos[:, None] * inv[None, :]
    cos, sin = jnp.cos(ang), jnp.sin(ang)
    reps = MXU_DIM // HEAD_DIM
    cosf = jnp.tile(jnp.concatenate([cos, cos], axis=1), (1, reps))
    sinf = jnp.tile(jnp.concatenate([-sin, sin], axis=1), (1, reps))
    return cosf, sinf


def _constants(nb):
    idx = np.arange(MXU_DIM)
    bd = (idx[:, None] // HEAD_DIM == idx[None, :] // HEAD_DIM).astype(np.float32) / HEAD_DIM
    rows = np.arange(nb * MOBA_HEADS)
    cols = np.arange(MOBA_WIDTH)
    headmask = (cols[None, :] // HEAD_DIM == rows[:, None] % MOBA_HEADS).astype(np.float32)
    return jnp.asarray(bd, BF16), jnp.asarray(headmask)


def kernel(x, attn_norm, w_in, q_norm_moba, k_norm_moba, q_norm_diff, k_norm_diff, lambda_q1, lambda_k1, lambda_q2, lambda_k2, moba_out_gain, diff_out_gain, w_out, ffn_norm, peer_query, peer_sub_keys, peer_down, peer_up):
    B, S, D = x.shape
    assert D == D_MODEL and S % MOBA_BLOCK == 0 and (B * S) % PEER_TM == 0
    nb = S // MOBA_BLOCK
    T = B * S
    x2d = x.reshape(T, D)
    cosf, sinf = _rope_tables(S)
    bd, headmask = _constants(nb)

    i = 0
    w = w_in[i]
    mqw, mkw, mvw, dqw, dkw, dvw = jnp.split(w, [512, 1024, 1536, 2048, 2560], axis=1)
    w_qk = jnp.concatenate([mqw, mkw, dqw, dkw], axis=1).astype(BF16)
    w_v_t = jnp.concatenate([mvw, dvw], axis=1).T.astype(BF16)
    scale = HEAD_DIM ** -0.5
    qk_gain = jnp.concatenate([
        jnp.tile(q_norm_moba[i], MOBA_HEADS) * scale,
        jnp.tile(k_norm_moba[i], MOBA_HEADS),
        jnp.tile(q_norm_diff[i].reshape(-1), DIFF_HEADS) * scale,
        jnp.tile(k_norm_diff[i].reshape(-1), DIFF_HEADS),
    ]).reshape(1, QK_COLS).astype(F32)

    proj, v_t, sel = _in_proj(x2d, attn_norm[i].reshape(1, D), w_qk, w_v_t, qk_gain, cosf, sinf,
                              bd, headmask, nb)
    a_out = _attention(proj, v_t, sel, moba_out_gain[i].reshape(1, MOBA_WIDTH), B, S, moba=True)
    lam_params = jnp.stack([lambda_q1[i], lambda_k1[i], lambda_q2[i], lambda_k2[i]]).astype(F32)
    b_out = _attention(proj, v_t, lam_params, diff_out_gain[i].reshape(1, DIFF_WIDTH), B, S, moba=False)

    w_o = w_out[i].astype(BF16)
    x2, xn = _out_proj(x2d, a_out, b_out, w_o[:MOBA_WIDTH], w_o[MOBA_WIDTH:], ffn_norm[i].reshape(1, D))

    wq_t = peer_query[i].T.astype(BF16)
    sub_keys = peer_sub_keys[i].reshape(2 * PEER_HEADS, PEER_NKEYS, PEER_HALF).astype(BF16)
    u = peer_down[i].astype(BF16)
    v_up_t = peer_up[i].T.astype(BF16)
    out = _peer(xn, x2, wq_t, sub_keys, u, v_up_t)
    return out.reshape(B, S, D)
```

```python
import functools
import math

import numpy as np
import jax
import jax.numpy as jnp
from jax import lax
from jax.experimental import pallas as pl
from jax.experimental.pallas import tpu as pltpu

D_MODEL = 1024
HEAD_DIM = 64
HALF_HEAD = HEAD_DIM // 2
MOBA_HEADS = 8
MOBA_WIDTH = MOBA_HEADS * HEAD_DIM
MOBA_BLOCK = 256
MOBA_TOPK = 3
DIFF_HEADS = 4
DIFF_V_DIM = 2 * HEAD_DIM
DIFF_WIDTH = DIFF_HEADS * DIFF_V_DIM
QK_COLS = 2 * MOBA_WIDTH + 2 * DIFF_WIDTH
V_COLS = MOBA_WIDTH + DIFF_WIDTH
N_CHAINS = 8
ROPE_THETA = 10000.0
EPS = 1e-6
NEG_INF = -1e30
LAMBDA_INIT = 0.8 - 0.6 * math.exp(-0.3 * 0)
PEER_HEADS = 8
PEER_NKEYS = 128
PEER_EXPERTS = PEER_NKEYS * PEER_NKEYS
PEER_HALF = 128
PEER_TOPK = 16

LANES = 128
F32_ROWS = 8
BF16_ROWS = 16
MXU_DIM = 256
VMEM_LIMIT = 56 * 1024 * 1024

F32 = jnp.float32
BF16 = jnp.bfloat16


def _nt_dot(a, b):
    return lax.dot_general(a, b, (((1,), (1,)), ((), ())), preferred_element_type=F32)


def _in_proj_kernel(x_ref, g_ref, w_ref, wvt_ref, qkg_ref, cos_ref, sin_ref, bd_ref, hm_ref,
                    proj_ref, vt_ref, sel_ref, km_ref, *, nb):
    j = pl.program_id(0) % nb

    xf = x_ref[...]
    h = (xf * lax.rsqrt(jnp.mean(xf * xf, axis=-1, keepdims=True) + EPS) * g_ref[...]).astype(BF16)
    proj = jnp.dot(h, w_ref[...], preferred_element_type=F32)
    vt_ref[0] = _nt_dot(wvt_ref[...], h).astype(BF16)

    lane = lax.broadcasted_iota(jnp.int32, (MOBA_BLOCK, MXU_DIM), 1)
    first_half = (lane % HEAD_DIM) < HALF_HEAD
    cosf = cos_ref[...]
    sinf = sin_ref[...]
    bd = bd_ref[...]
    roped = []
    for c in range(QK_COLS // MXU_DIM):
        p = proj[:, c * MXU_DIM:(c + 1) * MXU_DIM]
        sq = p * p
        hi = sq.astype(BF16)
        lo = (sq - hi.astype(F32)).astype(BF16)
        ms = jnp.dot(hi, bd, preferred_element_type=F32) + jnp.dot(lo, bd, preferred_element_type=F32)
        pn = p * lax.rsqrt(ms + EPS) * qkg_ref[:, c * MXU_DIM:(c + 1) * MXU_DIM]
        partner = jnp.where(first_half,
                            pltpu.roll(pn, MXU_DIM - HALF_HEAD, axis=1),
                            pltpu.roll(pn, HALF_HEAD, axis=1))
        r = pn * cosf + partner * sinf
        roped.append(r)
        proj_ref[:, c * MXU_DIM:(c + 1) * MXU_DIM] = r.astype(BF16)

    mq = jnp.concatenate(roped[0:2], axis=1)
    mk = jnp.concatenate(roped[2:4], axis=1)

    @pl.when(j == 0)
    def _():
        km_ref[...] = jnp.zeros_like(km_ref)

    km = km_ref[...]
    km_rep = jnp.concatenate(
        [jnp.broadcast_to(km[n:n + 1, :], (MOBA_HEADS, MOBA_WIDTH)) for n in range(nb)], axis=0)
    gate_w = km_rep * hm_ref[...]
    gate_t = lax.dot_general(gate_w, mq, (((1,), (1,)), ((), ())),
                             precision=lax.Precision.HIGHEST, preferred_element_type=F32)
    g = [jnp.where(n < j, gate_t[n * MOBA_HEADS:(n + 1) * MOBA_HEADS, :], NEG_INF) for n in range(nb)]
    sel_rows = []
    for n in range(nb):
        rank = jnp.zeros((MOBA_HEADS, MOBA_BLOCK), F32)
        for m in range(nb):
            if m < n:
                rank = rank + (g[m] >= g[n]).astype(F32)
            elif m > n:
                rank = rank + (g[m] > g[n]).astype(F32)
        sel_rows.append(jnp.where((rank < MOBA_TOPK) & (n < j), 1.0, 0.0))
    sel_ref[...] = jnp.concatenate(sel_rows, axis=0)

    km_ref[pl.ds(j, 1), :] = jnp.mean(mk, axis=0, keepdims=True)


def _in_proj(x2d, attn_gain, w_qk, w_v_t, qk_gain, cosf, sinf, bd, headmask, nb):
    T = x2d.shape[0]
    n_tiles = T // MOBA_BLOCK
    sel_rows = nb * MOBA_HEADS
    return pl.pallas_call(
        functools.partial(_in_proj_kernel, nb=nb),
        grid=(n_tiles,),
        in_specs=[
            pl.BlockSpec((MOBA_BLOCK, D_MODEL), lambda i: (i, 0)),
            pl.BlockSpec((1, D_MODEL), lambda i: (0, 0)),
            pl.BlockSpec((D_MODEL, QK_COLS), lambda i: (0, 0)),
            pl.BlockSpec((V_COLS, D_MODEL), lambda i: (0, 0)),
            pl.BlockSpec((1, QK_COLS), lambda i: (0, 0)),
            pl.BlockSpec((MOBA_BLOCK, MXU_DIM), lambda i: (i % nb, 0)),
            pl.BlockSpec((MOBA_BLOCK, MXU_DIM), lambda i: (i % nb, 0)),
            pl.BlockSpec((MXU_DIM, MXU_DIM), lambda i: (0, 0)),
            pl.BlockSpec((sel_rows, MOBA_WIDTH), lambda i: (0, 0)),
        ],
        out_specs=[
            pl.BlockSpec((MOBA_BLOCK, QK_COLS), lambda i: (i, 0)),
            pl.BlockSpec((1, V_COLS, MOBA_BLOCK), lambda i: (i, 0, 0)),
            pl.BlockSpec((sel_rows, MOBA_BLOCK), lambda i: (i, 0)),
        ],
        out_shape=[
            jax.ShapeDtypeStruct((T, QK_COLS), BF16),
            jax.ShapeDtypeStruct((n_tiles, V_COLS, MOBA_BLOCK), BF16),
            jax.ShapeDtypeStruct((n_tiles * sel_rows, MOBA_BLOCK), F32),
        ],
        scratch_shapes=[pltpu.VMEM((nb, MOBA_WIDTH), F32)],
        compiler_params=pltpu.CompilerParams(
            dimension_semantics=("arbitrary",), vmem_limit_bytes=VMEM_LIMIT),
        name="in_proj",
    )(x2d, attn_gain, w_qk, w_v_t, qk_gain, cosf, sinf, bd, headmask)


def _causal_mask_t(n):
    key = lax.broadcasted_iota(jnp.int32, (n, n), 0)
    qry = lax.broadcasted_iota(jnp.int32, (n, n), 1)
    return key <= qry


def _attn_region(first, k_blk, v_blk, mask_fn, q_ref, refs, v_rows, v_slice):
    s_ref, p_ref, m_ref, l_ref, alpha_ref, acc_ref = refs
    for c in range(N_CHAINS):
        cols = slice(c * HEAD_DIM, (c + 1) * HEAD_DIM)
        s = mask_fn(c, _nt_dot(k_blk[:, cols], q_ref[:, cols]))
        s_ref[c] = s
        mx = jnp.max(s, axis=0, keepdims=True)
        if first:
            m_ref[c:c + 1, :] = mx
        else:
            m_prev = m_ref[c:c + 1, :]
            m_new = jnp.maximum(m_prev, mx)
            alpha_ref[c:c + 1, :] = jnp.exp(m_prev - m_new)
            m_ref[c:c + 1, :] = m_new
    for c in range(N_CHAINS):
        p = jnp.exp(s_ref[c] - m_ref[c:c + 1, :])
        lsum = jnp.sum(p, axis=0, keepdims=True)
        if first:
            l_ref[c:c + 1, :] = lsum
        else:
            l_ref[c:c + 1, :] = alpha_ref[c:c + 1, :] * l_ref[c:c + 1, :] + lsum
        p_ref[c] = p.astype(BF16)
    for c in range(N_CHAINS):
        rows = slice(c * v_rows, (c + 1) * v_rows)
        pv = jnp.dot(v_blk[v_slice(c), :], p_ref[c], preferred_element_type=F32)
        if first:
            acc_ref[rows, :] = pv
        else:
            acc_ref[rows, :] = alpha_ref[c:c + 1, :] * acc_ref[rows, :] + pv


def _attn_kernel(q_ref, k_ref, vt_ref, *rest, nb, moba):
    if moba:
        sel_ref, gain_ref, o_ref, s_ref, p_ref, m_ref, l_ref, alpha_ref, acc_ref = rest
        v_rows, chains_per_v = HEAD_DIM, 1
    else:
        lam_ref, gain_ref, o_ref, s_ref, p_ref, m_ref, l_ref, alpha_ref, acc_ref = rest
        v_rows, chains_per_v = DIFF_V_DIM, 2
    refs = (s_ref, p_ref, m_ref, l_ref, alpha_ref, acc_ref)
    j = pl.program_id(1)
    own = pl.multiple_of(j * MOBA_BLOCK, MOBA_BLOCK)
    causal = _causal_mask_t(MOBA_BLOCK)

    def v_slice(c):
        hv = c // chains_per_v
        return slice(hv * v_rows, (hv + 1) * v_rows)

    _attn_region(True, k_ref[pl.ds(own, MOBA_BLOCK), :], vt_ref[j],
                 lambda c, s: jnp.where(causal, s, NEG_INF), q_ref, refs, v_rows, v_slice)

    for n in range(nb - 1):
        @pl.when(n < j)
        def _():
            def mask_fn(c, s):
                if not moba:
                    return s
                picked = sel_ref[n * MOBA_HEADS + c:n * MOBA_HEADS + c + 1, :] > 0.5
                return jnp.where(picked, s, NEG_INF)

            _attn_region(False, k_ref[n * MOBA_BLOCK:(n + 1) * MOBA_BLOCK, :], vt_ref[n],
                         mask_fn, q_ref, refs, v_rows, v_slice)

    if moba:
        for hp in range(MOBA_HEADS // 2):
            halves = []
            for c in (2 * hp, 2 * hp + 1):
                o = acc_ref[c * v_rows:(c + 1) * v_rows, :] / l_ref[c:c + 1, :]
                halves.append(o * lax.rsqrt(jnp.mean(o * o, axis=0, keepdims=True) + EPS))
            pair = jnp.concatenate(halves, axis=0)
            o_ref[:, hp * LANES:(hp + 1) * LANES] = (pair.T * gain_ref[:, hp * LANES:(hp + 1) * LANES]).astype(BF16)
    else:
        lp = lam_ref[...]
        lam = (jnp.exp(jnp.sum(lp[0:1] * lp[1:2], axis=1, keepdims=True))
               - jnp.exp(jnp.sum(lp[2:3] * lp[3:4], axis=1, keepdims=True)) + LAMBDA_INIT)
        for hd in range(DIFF_HEADS):
            o1 = acc_ref[(2 * hd) * v_rows:(2 * hd + 1) * v_rows, :] / l_ref[2 * hd:2 * hd + 1, :]
            o2 = acc_ref[(2 * hd + 1) * v_rows:(2 * hd + 2) * v_rows, :] / l_ref[2 * hd + 1:2 * hd + 2, :]
            o = o1 - lam * o2
            o = o * lax.rsqrt(jnp.mean(o * o, axis=0, keepdims=True) + EPS)
            gain = gain_ref[:, hd * LANES:(hd + 1) * LANES] * (1.0 - LAMBDA_INIT)
            o_ref[:, hd * LANES:(hd + 1) * LANES] = (o.T * gain).astype(BF16)


def _attention(proj, v_t, extra, gain, B, S, *, moba):
    nb = S // MOBA_BLOCK
    T = B * S
    width = N_CHAINS * HEAD_DIM
    qcol = 0 if moba else 2
    v_blk = 0 if moba else 1
    v_rows = HEAD_DIM if moba else DIFF_V_DIM
    if moba:
        extra_spec = pl.BlockSpec((nb * MOBA_HEADS, MOBA_BLOCK), lambda b, j: (b * nb + j, 0))
    else:
        extra_spec = pl.BlockSpec((4, HEAD_DIM), lambda b, j: (0, 0))
    return pl.pallas_call(
        functools.partial(_attn_kernel, nb=nb, moba=moba),
        grid=(B, nb),
        in_specs=[
            pl.BlockSpec((MOBA_BLOCK, width), lambda b, j: (b * nb + j, qcol)),
            pl.BlockSpec((S, width), lambda b, j: (b, qcol + 1)),
            pl.BlockSpec((nb, width, MOBA_BLOCK), lambda b, j: (b, v_blk, 0)),
            extra_spec,
            pl.BlockSpec((1, width), lambda b, j: (0, 0)),
        ],
        out_specs=pl.BlockSpec((MOBA_BLOCK, width), lambda b, j: (b * nb + j, 0)),
        out_shape=jax.ShapeDtypeStruct((T, width), BF16),
        scratch_shapes=[pltpu.VMEM((N_CHAINS, MOBA_BLOCK, MOBA_BLOCK), F32),
                        pltpu.VMEM((N_CHAINS, MOBA_BLOCK, MOBA_BLOCK), BF16),
                        pltpu.VMEM((N_CHAINS, MOBA_BLOCK), F32), pltpu.VMEM((N_CHAINS, MOBA_BLOCK), F32),
                        pltpu.VMEM((N_CHAINS, MOBA_BLOCK), F32),
                        pltpu.VMEM((N_CHAINS * v_rows, MOBA_BLOCK), F32)],
        compiler_params=pltpu.CompilerParams(
            dimension_semantics=("arbitrary", "arbitrary"), vmem_limit_bytes=VMEM_LIMIT),
        name="moba_attn" if moba else "diff_attn",
    )(proj, proj, v_t, extra, gain)


OUT_TM = 512


def _out_proj_kernel(x_ref, a_ref, b_ref, wa_ref, wb_ref, g_ref, x2_ref, xnt_ref):
    x2 = (x_ref[...] + jnp.dot(a_ref[...], wa_ref[...], preferred_element_type=F32)
          + jnp.dot(b_ref[...], wb_ref[...], preferred_element_type=F32))
    x2_ref[...] = x2
    xn = x2 * lax.rsqrt(jnp.mean(x2 * x2, axis=-1, keepdims=True) + EPS) * g_ref[...]
    xnt_ref[...] = xn.T.astype(BF16)


def _out_proj(x2d, a_out, b_out, w_a, w_b, ffn_gain):
    T = x2d.shape[0]
    return pl.pallas_call(
        _out_proj_kernel,
        grid=(T // OUT_TM,),
        in_specs=[
            pl.BlockSpec((OUT_TM, D_MODEL), lambda i: (i, 0)),
            pl.BlockSpec((OUT_TM, MOBA_WIDTH), lambda i: (i, 0)),
            pl.BlockSpec((OUT_TM, DIFF_WIDTH), lambda i: (i, 0)),
            pl.BlockSpec((MOBA_WIDTH, D_MODEL), lambda i: (0, 0)),
            pl.BlockSpec((DIFF_WIDTH, D_MODEL), lambda i: (0, 0)),
            pl.BlockSpec((1, D_MODEL), lambda i: (0, 0)),
        ],
        out_specs=[
            pl.BlockSpec((OUT_TM, D_MODEL), lambda i: (i, 0)),
            pl.BlockSpec((D_MODEL, OUT_TM), lambda i: (0, i)),
        ],
        out_shape=[
            jax.ShapeDtypeStruct((T, D_MODEL), F32),
            jax.ShapeDtypeStruct((D_MODEL, T), BF16),
        ],
        compiler_params=pltpu.CompilerParams(
            dimension_semantics=("arbitrary",), vmem_limit_bytes=VMEM_LIMIT),
        name="out_proj",
    )(x2d, a_out, b_out, w_a, w_b, ffn_gain)


PEER_TM = 512
PEER_ROWS = 16
PEER_TE = PEER_ROWS * PEER_NKEYS
ROWS_PER_DOT = MXU_DIM // PEER_NKEYS
N_TOP = PEER_TOPK + 1
CAND_PAIRS = [(a, b) for a in range(N_TOP) for b in range(N_TOP) if (a + 1) * (b + 1) <= N_TOP]
KEY_PIECES = PEER_NKEYS // BF16_ROWS


def _sorting_network(n):
    pairs = []
    p = 1
    while p < n:
        k = p
        while k >= 1:
            for j in range(k % p, n - k, 2 * k):
                for i in range(min(k, n - j - k)):
                    if (i + j) // (2 * p) == (i + j + k) // (2 * p):
                        pairs.append((i + j, i + j + k))
            k //= 2
        p *= 2
    return pairs


def _top_rows(s, k):
    n_groups = s.shape[0] // F32_ROWS
    v = [s[g * F32_ROWS:(g + 1) * F32_ROWS, :] for g in range(n_groups)]
    neg = jnp.full_like(v[0], -jnp.inf)
    while len(v) & (len(v) - 1):
        v.append(neg)
    for i, j in _sorting_network(len(v)):
        v[i], v[j] = jnp.maximum(v[i], v[j]), jnp.minimum(v[i], v[j])
    out = []
    for r in range(k):
        m = jnp.max(v[0], axis=0, keepdims=True)
        out.append(m)
        live = min(len(v), k - 1 - r)
        hit = v[0] >= m
        for p in range(live):
            v[p] = jnp.where(hit, v[p + 1] if p + 1 < len(v) else neg, v[p])
    return out


def _chain_select(x, thresholds, values, default):
    out = jnp.full_like(x, default)
    for t, val in zip(thresholds, values):
        out = jnp.where(x >= t, val, out)
    return out


def _row_bf16(ref, h, tile_start, row_in_tile, toks):
    tile = ref[h, pl.ds(tile_start, F32_ROWS), toks]
    row = tile[row_in_tile:row_in_tile + 1, :]
    return jnp.broadcast_to(row, (BF16_ROWS, row.shape[1])).astype(BF16)


def _peer_kernel(xnt_ref, x2_ref, wq_ref, sk_ref, u_ref, vt_ref, o_ref,
                 r2_ref, e2_ref, c_ref, e1_ref, acc_ref, wt_ref):
    e = pl.program_id(1)

    @pl.when(e == 0)
    def _():
        q_t = jnp.dot(wq_ref[...], xnt_ref[...], preferred_element_type=F32).astype(BF16)
        for h in range(PEER_HEADS):
            s = [jnp.dot(sk_ref[2 * h + p], q_t[(2 * h + p) * PEER_HALF:(2 * h + p + 1) * PEER_HALF, :],
                         preferred_element_type=F32) for p in range(2)]
            top1 = _top_rows(s[0], N_TOP)
            top2 = _top_rows(s[1], N_TOP)
            cands = [top1[a] + top2[b] for (a, b) in CAND_PAIRS]
            pad = (-len(cands)) % F32_ROWS
            c_all = jnp.concatenate(cands + [jnp.full_like(cands[0], -jnp.inf)] * pad, axis=0)
            best = _top_rows(c_all, N_TOP)
            thr = 0.5 * (best[PEER_TOPK - 1] + best[PEER_TOPK])
            z = jnp.ones_like(best[0])
            for r in range(1, PEER_TOPK):
                z = z + jnp.exp(best[r] - best[0])
            order = range(PEER_TOPK)
            rank2 = _chain_select(s[1], [top2[j] for j in reversed(order)],
                                  [float(j) for j in reversed(order)], float(PEER_TOPK))
            count = _chain_select(s[0], [thr - top2[j] for j in order], [float(j + 1) for j in order], 0.0)
            r2_ref[h] = rank2.astype(BF16)
            e2_ref[h] = (jnp.exp(s[1] - top2[0]) / z).astype(BF16)
            e1_ref[h] = jnp.exp(s[0] - top1[0])
            c_ref[h] = count

    n_chunks = PEER_ROWS // ROWS_PER_DOT
    halves = PEER_TM // MXU_DIM
    units = [(cb, hf) for cb in range(n_chunks) for hf in range(halves)]

    def scores(cb, hf):
        return jnp.dot(u_ref[cb * MXU_DIM:(cb + 1) * MXU_DIM, :],
                       xnt_ref[:, hf * MXU_DIM:(hf + 1) * MXU_DIM],
                       preferred_element_type=F32)

    def gate_weights(cb, hf, a_t):
        toks = slice(hf * MXU_DIM, (hf + 1) * MXU_DIM)
        for rr in range(ROWS_PER_DOT):
            row = cb * ROWS_PER_DOT + rr
            tile_start = pl.multiple_of(e * PEER_ROWS + (row // F32_ROWS) * F32_ROWS, F32_ROWS)
            g = [jnp.zeros((BF16_ROWS, MXU_DIM), BF16) for _ in range(KEY_PIECES)]
            for h in range(PEER_HEADS):
                c_b = _row_bf16(c_ref, h, tile_start, row % F32_ROWS, toks)
                e1_b = _row_bf16(e1_ref, h, tile_start, row % F32_ROWS, toks)
                for k in range(KEY_PIECES):
                    piece = slice(k * BF16_ROWS, (k + 1) * BF16_ROWS)
                    keep = r2_ref[h, piece, toks] < c_b
                    g[k] = g[k] + jnp.where(keep, e2_ref[h, piece, toks] * e1_b, jnp.zeros_like(e1_b))
            a = a_t[rr * PEER_NKEYS:(rr + 1) * PEER_NKEYS, :]
            act = (a + a * lax.erf(a * math.sqrt(0.5))).astype(BF16)
            base = (cb * ROWS_PER_DOT + rr) * PEER_NKEYS
            for k in range(KEY_PIECES):
                wt_ref[base + k * BF16_ROWS:base + (k + 1) * BF16_ROWS, toks] = (
                    g[k] * act[k * BF16_ROWS:(k + 1) * BF16_ROWS, :])

    def up_project(cb, hf):
        return jnp.dot(vt_ref[:, cb * MXU_DIM:(cb + 1) * MXU_DIM],
                       wt_ref[cb * MXU_DIM:(cb + 1) * MXU_DIM, hf * MXU_DIM:(hf + 1) * MXU_DIM],
                       preferred_element_type=F32)

    parts = [None] * halves
    a_t = {units[0]: scores(*units[0])}
    for ui, (cb, hf) in enumerate(units):
        if ui + 1 < len(units):
            a_t[units[ui + 1]] = scores(*units[ui + 1])
        gate_weights(cb, hf, a_t.pop((cb, hf)))
        up = up_project(cb, hf)
        parts[hf] = up if parts[hf] is None else parts[hf] + up

    @pl.when(e == 0)
    def _():
        acc_ref[...] = jnp.zeros_like(acc_ref)

    for hf in range(halves):
        acc_ref[:, hf * MXU_DIM:(hf + 1) * MXU_DIM] += parts[hf]

    @pl.when(e == pl.num_programs(1) - 1)
    def _():
        o_ref[...] = x2_ref[...] + acc_ref[...].T


def _peer(xn_t, x2, wq_t, sub_keys, u, v_t):
    T = xn_t.shape[1]
    head_f32 = pltpu.VMEM((PEER_HEADS, PEER_NKEYS, PEER_TM), F32)
    head_bf16 = pltpu.VMEM((PEER_HEADS, PEER_NKEYS, PEER_TM), BF16)
    return pl.pallas_call(
        _peer_kernel,
        grid=(T // PEER_TM, PEER_EXPERTS // PEER_TE),
        in_specs=[
            pl.BlockSpec((D_MODEL, PEER_TM), lambda i, e: (0, i)),
            pl.BlockSpec((PEER_TM, D_MODEL), lambda i, e: (i, 0)),
            pl.BlockSpec((2 * PEER_HEADS * PEER_HALF, D_MODEL), lambda i, e: (0, 0)),
            pl.BlockSpec((2 * PEER_HEADS, PEER_NKEYS, PEER_HALF), lambda i, e: (0, 0, 0)),
            pl.BlockSpec((PEER_TE, D_MODEL), lambda i, e: (e, 0)),
            pl.BlockSpec((D_MODEL, PEER_TE), lambda i, e: (0, e)),
        ],
        out_specs=pl.BlockSpec((PEER_TM, D_MODEL), lambda i, e: (i, 0)),
        out_shape=jax.ShapeDtypeStruct((T, D_MODEL), F32),
        scratch_shapes=[head_bf16, head_bf16, head_f32, head_f32,
                        pltpu.VMEM((D_MODEL, PEER_TM), F32),
                        pltpu.VMEM((PEER_TE, PEER_TM), BF16)],
        compiler_params=pltpu.CompilerParams(
            dimension_semantics=("arbitrary", "arbitrary"), vmem_limit_bytes=VMEM_LIMIT),
        name="peer",
    )(xn_t, x2, wq_t, sub_keys, u, v_t)


def _rope_tables(S):
    pos = jnp.arange(S, dtype=F32)
    inv = 1.0 / (ROPE_THETA ** (jnp.arange(0, HEAD_DIM, 2, dtype=F32) / HEAD_DIM))
    ang = pos[:, None] * inv[None, :]
    cos, sin = jnp.cos(ang), jnp.sin(ang)
    reps = MXU_DIM // HEAD_DIM
    cosf = jnp.tile(jnp.concatenate([cos, cos], axis=1), (1, reps))
    sinf = jnp.tile(jnp.concatenate([-sin, sin], axis=1), (1, reps))
    return cosf, sinf


def _constants(nb):
    idx = np.arange(MXU_DIM)
    bd = (idx[:, None] // HEAD_DIM == idx[None, :] // HEAD_DIM).astype(np.float32) / HEAD_DIM
    rows = np.arange(nb * MOBA_HEADS)
    cols = np.arange(MOBA_WIDTH)
    headmask = (cols[None, :] // HEAD_DIM == rows[:, None] % MOBA_HEADS).astype(np.float32)
    return jnp.asarray(bd, BF16), jnp.asarray(headmask)


def kernel(x, attn_norm, w_in, q_norm_moba, k_norm_moba, q_norm_diff, k_norm_diff, lambda_q1, lambda_k1, lambda_q2, lambda_k2, moba_out_gain, diff_out_gain, w_out, ffn_norm, peer_query, peer_sub_keys, peer_down, peer_up):
    B, S, D = x.shape
    assert D == D_MODEL and S % MOBA_BLOCK == 0 and (B * S) % PEER_TM == 0
    nb = S // MOBA_BLOCK
    T = B * S
    x2d = x.reshape(T, D)
    cosf, sinf = _rope_tables(S)
    bd, headmask = _constants(nb)

    i = 0
    w = w_in[i]
    mqw, mkw, mvw, dqw, dkw, dvw = jnp.split(w, [512, 1024, 1536, 2048, 2560], axis=1)
    w_qk = jnp.concatenate([mqw, mkw, dqw, dkw], axis=1).astype(BF16)
    w_v_t = jnp.concatenate([mvw, dvw], axis=1).T.astype(BF16)
    scale = HEAD_DIM ** -0.5
    qk_gain = jnp.concatenate([
        jnp.tile(q_norm_moba[i], MOBA_HEADS) * scale,
        jnp.tile(k_norm_moba[i], MOBA_HEADS),
        jnp.tile(q_norm_diff[i].reshape(-1), DIFF_HEADS) * scale,
        jnp.tile(k_norm_diff[i].reshape(-1), DIFF_HEADS),
    ]).reshape(1, QK_COLS).astype(F32)

    proj, v_t, sel = _in_proj(x2d, attn_norm[i].reshape(1, D), w_qk, w_v_t, qk_gain, cosf, sinf,
                              bd, headmask, nb)
    a_out = _attention(proj, v_t, sel, moba_out_gain[i].reshape(1, MOBA_WIDTH), B, S, moba=True)
    lam_params = jnp.stack([lambda_q1[i], lambda_k1[i], lambda_q2[i], lambda_k2[i]]).astype(F32)
    b_out = _attention(proj, v_t, lam_params, diff_out_gain[i].reshape(1, DIFF_WIDTH), B, S, moba=False)

    w_o = w_out[i].astype(BF16)
    x2, xn_t = _out_proj(x2d, a_out, b_out, w_o[:MOBA_WIDTH], w_o[MOBA_WIDTH:], ffn_norm[i].reshape(1, D))

    wq_t = peer_query[i].T.astype(BF16)
    sub_keys = peer_sub_keys[i].reshape(2 * PEER_HEADS, PEER_NKEYS, PEER_HALF).astype(BF16)
    u = peer_down[i].astype(BF16)
    v_up_t = (0.5 * peer_up[i]).T.astype(BF16)
    out = _peer(xn_t, x2, wq_t, sub_keys, u, v_up_t)
    return out.reshape(B, S, D)
```

```python
import functools
import math

import numpy as np
import jax
import jax.numpy as jnp
from jax import lax
from jax.experimental import pallas as pl
from jax.experimental.pallas import tpu as pltpu

D_MODEL = 1024
HEAD_DIM = 64
HALF_HEAD = HEAD_DIM // 2
MOBA_HEADS = 8
MOBA_WIDTH = MOBA_HEADS * HEAD_DIM
MOBA_BLOCK = 256
MOBA_TOPK = 3
DIFF_HEADS = 4
DIFF_V_DIM = 2 * HEAD_DIM
DIFF_WIDTH = DIFF_HEADS * DIFF_V_DIM
QK_COLS = 2 * MOBA_WIDTH + 2 * DIFF_WIDTH
V_COLS = MOBA_WIDTH + DIFF_WIDTH
N_CHAINS = 8
ATTN_ORDER = ([[("s", c) for c in range(N_CHAINS)]] + [[("p", c) for c in range(N_CHAINS)]]
              + [[("v", c) for c in range(N_CHAINS)]])
ROPE_THETA = 10000.0
EPS = 1e-6
NEG_INF = -1e30
LAMBDA_INIT = 0.8 - 0.6 * math.exp(-0.3 * 0)
PEER_HEADS = 8
PEER_NKEYS = 128
PEER_EXPERTS = PEER_NKEYS * PEER_NKEYS
PEER_HALF = 128
PEER_TOPK = 16

LANES = 128
F32_ROWS = 8
BF16_ROWS = 16
MXU_DIM = 256
VMEM_LIMIT = 56 * 1024 * 1024

F32 = jnp.float32
BF16 = jnp.bfloat16


def _nt_dot(a, b):
    return lax.dot_general(a, b, (((1,), (1,)), ((), ())), preferred_element_type=F32)


def _in_proj_kernel(x_ref, g_ref, w_ref, wvt_ref, qkg_ref, cos_ref, sin_ref, bd_ref, hm_ref,
                    proj_ref, vt_ref, sel_ref, km_ref, *, nb):
    j = pl.program_id(0) % nb

    xf = x_ref[...]
    h = (xf * lax.rsqrt(jnp.mean(xf * xf, axis=-1, keepdims=True) + EPS) * g_ref[...]).astype(BF16)
    proj = jnp.dot(h, w_ref[...], preferred_element_type=F32)
    vt_ref[0] = _nt_dot(wvt_ref[...], h).astype(BF16)

    lane = lax.broadcasted_iota(jnp.int32, (MOBA_BLOCK, MXU_DIM), 1)
    first_half = (lane % HEAD_DIM) < HALF_HEAD
    cosf = cos_ref[...]
    sinf = sin_ref[...]
    bd = bd_ref[...]
    roped = []
    for c in range(QK_COLS // MXU_DIM):
        p = proj[:, c * MXU_DIM:(c + 1) * MXU_DIM]
        sq = p * p
        hi = sq.astype(BF16)
        lo = (sq - hi.astype(F32)).astype(BF16)
        ms = jnp.dot(hi, bd, preferred_element_type=F32) + jnp.dot(lo, bd, preferred_element_type=F32)
        pn = p * lax.rsqrt(ms + EPS) * qkg_ref[:, c * MXU_DIM:(c + 1) * MXU_DIM]
        partner = jnp.where(first_half,
                            pltpu.roll(pn, MXU_DIM - HALF_HEAD, axis=1),
                            pltpu.roll(pn, HALF_HEAD, axis=1))
        r = pn * cosf + partner * sinf
        roped.append(r)
        proj_ref[:, c * MXU_DIM:(c + 1) * MXU_DIM] = r.astype(BF16)

    mq = jnp.concatenate(roped[0:2], axis=1)
    mk = jnp.concatenate(roped[2:4], axis=1)

    @pl.when(j == 0)
    def _():
        km_ref[...] = jnp.zeros_like(km_ref)

    km = km_ref[...]
    km_rep = jnp.concatenate(
        [jnp.broadcast_to(km[n:n + 1, :], (MOBA_HEADS, MOBA_WIDTH)) for n in range(nb)], axis=0)
    gate_w = km_rep * hm_ref[...]
    gate_t = lax.dot_general(gate_w, mq, (((1,), (1,)), ((), ())),
                             precision=lax.Precision.HIGHEST, preferred_element_type=F32)
    g = [jnp.where(n < j, gate_t[n * MOBA_HEADS:(n + 1) * MOBA_HEADS, :], NEG_INF) for n in range(nb)]
    sel_rows = []
    for n in range(nb):
        rank = jnp.zeros((MOBA_HEADS, MOBA_BLOCK), F32)
        for m in range(nb):
            if m < n:
                rank = rank + (g[m] >= g[n]).astype(F32)
            elif m > n:
                rank = rank + (g[m] > g[n]).astype(F32)
        sel_rows.append(jnp.where((rank < MOBA_TOPK) & (n < j), 1.0, 0.0))
    sel_ref[...] = jnp.concatenate(sel_rows, axis=0)

    km_ref[pl.ds(j, 1), :] = jnp.mean(mk, axis=0, keepdims=True)


def _in_proj(x2d, attn_gain, w_qk, w_v_t, qk_gain, cosf, sinf, bd, headmask, nb):
    T = x2d.shape[0]
    n_tiles = T // MOBA_BLOCK
    sel_rows = nb * MOBA_HEADS
    return pl.pallas_call(
        functools.partial(_in_proj_kernel, nb=nb),
        grid=(n_tiles,),
        in_specs=[
            pl.BlockSpec((MOBA_BLOCK, D_MODEL), lambda i: (i, 0)),
            pl.BlockSpec((1, D_MODEL), lambda i: (0, 0)),
            pl.BlockSpec((D_MODEL, QK_COLS), lambda i: (0, 0)),
            pl.BlockSpec((V_COLS, D_MODEL), lambda i: (0, 0)),
            pl.BlockSpec((1, QK_COLS), lambda i: (0, 0)),
            pl.BlockSpec((MOBA_BLOCK, MXU_DIM), lambda i: (i % nb, 0)),
            pl.BlockSpec((MOBA_BLOCK, MXU_DIM), lambda i: (i % nb, 0)),
            pl.BlockSpec((MXU_DIM, MXU_DIM), lambda i: (0, 0)),
            pl.BlockSpec((sel_rows, MOBA_WIDTH), lambda i: (0, 0)),
        ],
        out_specs=[
            pl.BlockSpec((MOBA_BLOCK, QK_COLS), lambda i: (i, 0)),
            pl.BlockSpec((1, V_COLS, MOBA_BLOCK), lambda i: (i, 0, 0)),
            pl.BlockSpec((sel_rows, MOBA_BLOCK), lambda i: (i, 0)),
        ],
        out_shape=[
            jax.ShapeDtypeStruct((T, QK_COLS), BF16),
            jax.ShapeDtypeStruct((n_tiles, V_COLS, MOBA_BLOCK), BF16),
            jax.ShapeDtypeStruct((n_tiles * sel_rows, MOBA_BLOCK), F32),
        ],
        scratch_shapes=[pltpu.VMEM((nb, MOBA_WIDTH), F32)],
        compiler_params=pltpu.CompilerParams(
            dimension_semantics=("arbitrary",), vmem_limit_bytes=VMEM_LIMIT),
        name="in_proj",
    )(x2d, attn_gain, w_qk, w_v_t, qk_gain, cosf, sinf, bd, headmask)


def _causal_mask_t(n):
    key = lax.broadcasted_iota(jnp.int32, (n, n), 0)
    qry = lax.broadcasted_iota(jnp.int32, (n, n), 1)
    return key <= qry


def _attn_region(first, k_blk, v_blk, mask_fn, q_ref, refs, v_rows, v_slice):
    s_ref, p_ref, m_ref, l_ref, alpha_ref, acc_ref = refs

    def scores(c):
        cols = slice(c * HEAD_DIM, (c + 1) * HEAD_DIM)
        s = mask_fn(c, _nt_dot(k_blk[:, cols], q_ref[:, cols]))
        s_ref[c] = s
        mx = jnp.max(s, axis=0, keepdims=True)
        if first:
            m_ref[c:c + 1, :] = mx
        else:
            m_prev = m_ref[c:c + 1, :]
            m_new = jnp.maximum(m_prev, mx)
            alpha_ref[c:c + 1, :] = jnp.exp2(m_prev - m_new)
            m_ref[c:c + 1, :] = m_new

    def probabilities(c):
        p = jnp.exp2(s_ref[c] - m_ref[c:c + 1, :])
        lsum = jnp.sum(p, axis=0, keepdims=True)
        if first:
            l_ref[c:c + 1, :] = lsum
        else:
            l_ref[c:c + 1, :] = alpha_ref[c:c + 1, :] * l_ref[c:c + 1, :] + lsum
        p_ref[c] = p.astype(BF16)

    def values(c):
        rows = slice(c * v_rows, (c + 1) * v_rows)
        pv = jnp.dot(v_blk[v_slice(c), :], p_ref[c], preferred_element_type=F32)
        if first:
            acc_ref[rows, :] = pv
        else:
            acc_ref[rows, :] = alpha_ref[c:c + 1, :] * acc_ref[rows, :] + pv

    for step in ATTN_ORDER:
        for kind, c in step:
            {"s": scores, "p": probabilities, "v": values}[kind](c)


def _attn_kernel(q_ref, k_ref, vt_ref, *rest, nb, moba):
    if moba:
        sel_ref, gain_ref, o_ref, s_ref, p_ref, m_ref, l_ref, alpha_ref, acc_ref = rest
        v_rows, chains_per_v = HEAD_DIM, 1
    else:
        lam_ref, gain_ref, o_ref, s_ref, p_ref, m_ref, l_ref, alpha_ref, acc_ref = rest
        v_rows, chains_per_v = DIFF_V_DIM, 2
    refs = (s_ref, p_ref, m_ref, l_ref, alpha_ref, acc_ref)
    j = pl.program_id(1)
    own = pl.multiple_of(j * MOBA_BLOCK, MOBA_BLOCK)
    causal = _causal_mask_t(MOBA_BLOCK)

    def v_slice(c):
        hv = c // chains_per_v
        return slice(hv * v_rows, (hv + 1) * v_rows)

    _attn_region(True, k_ref[pl.ds(own, MOBA_BLOCK), :], vt_ref[j],
                 lambda c, s: jnp.where(causal, s, NEG_INF), q_ref, refs, v_rows, v_slice)

    for n in range(nb - 1):
        @pl.when(n < j)
        def _():
            def mask_fn(c, s):
                if not moba:
                    return s
                picked = sel_ref[n * MOBA_HEADS + c:n * MOBA_HEADS + c + 1, :] > 0.5
                return jnp.where(picked, s, NEG_INF)

            _attn_region(False, k_ref[n * MOBA_BLOCK:(n + 1) * MOBA_BLOCK, :], vt_ref[n],
                         mask_fn, q_ref, refs, v_rows, v_slice)

    if moba:
        for hp in range(MOBA_HEADS // 2):
            halves = []
            for c in (2 * hp, 2 * hp + 1):
                o = acc_ref[c * v_rows:(c + 1) * v_rows, :] / l_ref[c:c + 1, :]
                halves.append(o * lax.rsqrt(jnp.mean(o * o, axis=0, keepdims=True) + EPS))
            pair = jnp.concatenate(halves, axis=0)
            o_ref[:, hp * LANES:(hp + 1) * LANES] = (pair.T * gain_ref[:, hp * LANES:(hp + 1) * LANES]).astype(BF16)
    else:
        lp = lam_ref[...]
        lam = (jnp.exp(jnp.sum(lp[0:1] * lp[1:2], axis=1, keepdims=True))
               - jnp.exp(jnp.sum(lp[2:3] * lp[3:4], axis=1, keepdims=True)) + LAMBDA_INIT)
        for hd in range(DIFF_HEADS):
            o1 = acc_ref[(2 * hd) * v_rows:(2 * hd + 1) * v_rows, :] / l_ref[2 * hd:2 * hd + 1, :]
            o2 = acc_ref[(2 * hd + 1) * v_rows:(2 * hd + 2) * v_rows, :] / l_ref[2 * hd + 1:2 * hd + 2, :]
            o = o1 - lam * o2
            o = o * lax.rsqrt(jnp.mean(o * o, axis=0, keepdims=True) + EPS)
            gain = gain_ref[:, hd * LANES:(hd + 1) * LANES] * (1.0 - LAMBDA_INIT)
            o_ref[:, hd * LANES:(hd + 1) * LANES] = (o.T * gain).astype(BF16)


def _attention(proj, v_t, extra, gain, B, S, *, moba):
    nb = S // MOBA_BLOCK
    T = B * S
    width = N_CHAINS * HEAD_DIM
    qcol = 0 if moba else 2
    v_blk = 0 if moba else 1
    v_rows = HEAD_DIM if moba else DIFF_V_DIM
    if moba:
        extra_spec = pl.BlockSpec((nb * MOBA_HEADS, MOBA_BLOCK), lambda b, j: (b * nb + j, 0))
    else:
        extra_spec = pl.BlockSpec((4, HEAD_DIM), lambda b, j: (0, 0))
    return pl.pallas_call(
        functools.partial(_attn_kernel, nb=nb, moba=moba),
        grid=(B, nb),
        in_specs=[
            pl.BlockSpec((MOBA_BLOCK, width), lambda b, j: (b * nb + j, qcol)),
            pl.BlockSpec((S, width), lambda b, j: (b, qcol + 1)),
            pl.BlockSpec((nb, width, MOBA_BLOCK), lambda b, j: (b, v_blk, 0)),
            extra_spec,
            pl.BlockSpec((1, width), lambda b, j: (0, 0)),
        ],
        out_specs=pl.BlockSpec((MOBA_BLOCK, width), lambda b, j: (b * nb + j, 0)),
        out_shape=jax.ShapeDtypeStruct((T, width), BF16),
        scratch_shapes=[pltpu.VMEM((N_CHAINS, MOBA_BLOCK, MOBA_BLOCK), F32),
                        pltpu.VMEM((N_CHAINS, MOBA_BLOCK, MOBA_BLOCK), BF16),
                        pltpu.VMEM((N_CHAINS, MOBA_BLOCK), F32), pltpu.VMEM((N_CHAINS, MOBA_BLOCK), F32),
                        pltpu.VMEM((N_CHAINS, MOBA_BLOCK), F32),
                        pltpu.VMEM((N_CHAINS * v_rows, MOBA_BLOCK), F32)],
        compiler_params=pltpu.CompilerParams(
            dimension_semantics=("arbitrary", "arbitrary"), vmem_limit_bytes=VMEM_LIMIT),
        name="moba_attn" if moba else "diff_attn",
    )(proj, proj, v_t, extra, gain)


OUT_TM = 512


def _out_proj_kernel(x_ref, a_ref, b_ref, wa_ref, wb_ref, g_ref, x2_ref, xnt_ref):
    x2 = (x_ref[...] + jnp.dot(a_ref[...], wa_ref[...], preferred_element_type=F32)
          + jnp.dot(b_ref[...], wb_ref[...], preferred_element_type=F32))
    x2_ref[...] = x2
    xn = x2 * lax.rsqrt(jnp.mean(x2 * x2, axis=-1, keepdims=True) + EPS) * g_ref[...]
    xnt_ref[...] = xn.T.astype(BF16)


def _out_proj(x2d, a_out, b_out, w_a, w_b, ffn_gain):
    T = x2d.shape[0]
    return pl.pallas_call(
        _out_proj_kernel,
        grid=(T // OUT_TM,),
        in_specs=[
            pl.BlockSpec((OUT_TM, D_MODEL), lambda i: (i, 0)),
            pl.BlockSpec((OUT_TM, MOBA_WIDTH), lambda i: (i, 0)),
            pl.BlockSpec((OUT_TM, DIFF_WIDTH), lambda i: (i, 0)),
            pl.BlockSpec((MOBA_WIDTH, D_MODEL), lambda i: (0, 0)),
            pl.BlockSpec((DIFF_WIDTH, D_MODEL), lambda i: (0, 0)),
            pl.BlockSpec((1, D_MODEL), lambda i: (0, 0)),
        ],
        out_specs=[
            pl.BlockSpec((OUT_TM, D_MODEL), lambda i: (i, 0)),
            pl.BlockSpec((D_MODEL, OUT_TM), lambda i: (0, i)),
        ],
        out_shape=[
            jax.ShapeDtypeStruct((T, D_MODEL), F32),
            jax.ShapeDtypeStruct((D_MODEL, T), BF16),
        ],
        compiler_params=pltpu.CompilerParams(
            dimension_semantics=("arbitrary",), vmem_limit_bytes=VMEM_LIMIT),
        name="out_proj",
    )(x2d, a_out, b_out, w_a, w_b, ffn_gain)


PEER_TM = 512
PEER_ROWS = 16
PEER_TE = PEER_ROWS * PEER_NKEYS
ROWS_PER_DOT = MXU_DIM // PEER_NKEYS
N_TOP = PEER_TOPK + 1
CAND_PAIRS = [(a, b) for a in range(N_TOP) for b in range(N_TOP) if (a + 1) * (b + 1) <= N_TOP]
KEY_PIECES = PEER_NKEYS // BF16_ROWS
UP_CHUNKS = 2
SCORE_LOOKAHEAD = 2


def _sorting_network(n):
    pairs = []
    p = 1
    while p < n:
        k = p
        while k >= 1:
            for j in range(k % p, n - k, 2 * k):
                for i in range(min(k, n - j - k)):
                    if (i + j) // (2 * p) == (i + j + k) // (2 * p):
                        pairs.append((i + j, i + j + k))
            k //= 2
        p *= 2
    return pairs


def _top_rows(s, k):
    n_groups = s.shape[0] // F32_ROWS
    v = [s[g * F32_ROWS:(g + 1) * F32_ROWS, :] for g in range(n_groups)]
    neg = jnp.full_like(v[0], -jnp.inf)
    while len(v) & (len(v) - 1):
        v.append(neg)
    for i, j in _sorting_network(len(v)):
        v[i], v[j] = jnp.maximum(v[i], v[j]), jnp.minimum(v[i], v[j])
    out = []
    for r in range(k):
        m = jnp.max(v[0], axis=0, keepdims=True)
        out.append(m)
        live = min(len(v), k - 1 - r)
        hit = v[0] >= m
        for p in range(live):
            v[p] = jnp.where(hit, v[p + 1] if p + 1 < len(v) else neg, v[p])
    return out


def _chain_select(x, thresholds, values, default):
    out = jnp.full_like(x, default)
    for t, val in zip(thresholds, values):
        out = jnp.where(x >= t, val, out)
    return out


def _row_bf16(ref, h, tile_start, row_in_tile, toks):
    tile = ref[h, pl.ds(tile_start, F32_ROWS), toks]
    row = tile[row_in_tile:row_in_tile + 1, :]
    return jnp.broadcast_to(row, (BF16_ROWS, row.shape[1])).astype(BF16)


def _peer_kernel(xnt_ref, x2_ref, wq_ref, sk_ref, u_ref, vt_ref, o_ref,
                 r2_ref, e2_ref, c_ref, e1_ref, acc_ref, wt_ref):
    e = pl.program_id(1)

    @pl.when(e == 0)
    def _():
        q_t = jnp.dot(wq_ref[...], xnt_ref[...], preferred_element_type=F32).astype(BF16)
        for h in range(PEER_HEADS):
            s = [jnp.dot(sk_ref[2 * h + p], q_t[(2 * h + p) * PEER_HALF:(2 * h + p + 1) * PEER_HALF, :],
                         preferred_element_type=F32) for p in range(2)]
            top1 = _top_rows(s[0], N_TOP)
            top2 = _top_rows(s[1], N_TOP)
            cands = [top1[a] + top2[b] for (a, b) in CAND_PAIRS]
            pad = (-len(cands)) % F32_ROWS
            c_all = jnp.concatenate(cands + [jnp.full_like(cands[0], -jnp.inf)] * pad, axis=0)
            best = _top_rows(c_all, N_TOP)
            thr = 0.5 * (best[PEER_TOPK - 1] + best[PEER_TOPK])
            z = jnp.ones_like(best[0])
            for r in range(1, PEER_TOPK):
                z = z + jnp.exp(best[r] - best[0])
            order = range(PEER_TOPK)
            rank2 = _chain_select(s[1], [top2[j] for j in reversed(order)],
                                  [float(j) for j in reversed(order)], float(PEER_TOPK))
            count = _chain_select(s[0], [thr - top2[j] for j in order], [float(j + 1) for j in order], 0.0)
            r2_ref[h] = rank2.astype(BF16)
            e2_ref[h] = (jnp.exp(s[1] - top2[0]) / z).astype(BF16)
            e1_ref[h] = jnp.exp(s[0] - top1[0])
            c_ref[h] = count

    n_chunks = PEER_ROWS // ROWS_PER_DOT
    halves = PEER_TM // MXU_DIM
    units = [(cb, hf) for cb in range(n_chunks) for hf in range(halves)]

    def scores(cb, hf):
        return jnp.dot(u_ref[cb * MXU_DIM:(cb + 1) * MXU_DIM, :],
                       xnt_ref[:, hf * MXU_DIM:(hf + 1) * MXU_DIM],
                       preferred_element_type=F32)

    def gate_weights(cb, hf, a_t):
        toks = slice(hf * MXU_DIM, (hf + 1) * MXU_DIM)
        for rr in range(ROWS_PER_DOT):
            row = cb * ROWS_PER_DOT + rr
            tile_start = pl.multiple_of(e * PEER_ROWS + (row // F32_ROWS) * F32_ROWS, F32_ROWS)
            c_b = [_row_bf16(c_ref, h, tile_start, row % F32_ROWS, toks) for h in range(PEER_HEADS)]
            e1_b = [_row_bf16(e1_ref, h, tile_start, row % F32_ROWS, toks) for h in range(PEER_HEADS)]
            a = a_t[rr * PEER_NKEYS:(rr + 1) * PEER_NKEYS, :]
            act = (a + a * lax.erf(a * math.sqrt(0.5))).astype(BF16)
            base = (cb * ROWS_PER_DOT + rr) * PEER_NKEYS
            for k in range(KEY_PIECES):
                piece = slice(k * BF16_ROWS, (k + 1) * BF16_ROWS)
                g = jnp.zeros((BF16_ROWS, MXU_DIM), BF16)
                for h in range(PEER_HEADS):
                    keep = r2_ref[h, piece, toks] < c_b[h]
                    g = g + jnp.where(keep, e2_ref[h, piece, toks] * e1_b[h], jnp.zeros_like(g))
                wt_ref[base + k * BF16_ROWS:base + (k + 1) * BF16_ROWS, toks] = g * act[piece, :]

    def up_project(cb, hf):
        rows = slice((cb + 1 - UP_CHUNKS) * MXU_DIM, (cb + 1) * MXU_DIM)
        toks = slice(hf * MXU_DIM, (hf + 1) * MXU_DIM)
        acc_ref[:, toks] += jnp.dot(vt_ref[:, rows], wt_ref[rows, toks], preferred_element_type=F32)

    @pl.when(e == 0)
    def _():
        acc_ref[...] = jnp.zeros_like(acc_ref)

    a_t = {u: scores(*u) for u in units[:SCORE_LOOKAHEAD]}
    for ui, (cb, hf) in enumerate(units):
        if ui + SCORE_LOOKAHEAD < len(units):
            a_t[units[ui + SCORE_LOOKAHEAD]] = scores(*units[ui + SCORE_LOOKAHEAD])
        gate_weights(cb, hf, a_t.pop((cb, hf)))
        if (cb + 1) % UP_CHUNKS == 0:
            up_project(cb, hf)

    @pl.when(e == pl.num_programs(1) - 1)
    def _():
        o_ref[...] = x2_ref[...] + acc_ref[...].T


def _peer(xn_t, x2, wq_t, sub_keys, u, v_t):
    T = xn_t.shape[1]
    head_f32 = pltpu.VMEM((PEER_HEADS, PEER_NKEYS, PEER_TM), F32)
    head_bf16 = pltpu.VMEM((PEER_HEADS, PEER_NKEYS, PEER_TM), BF16)
    return pl.pallas_call(
        _peer_kernel,
        grid=(T // PEER_TM, PEER_EXPERTS // PEER_TE),
        in_specs=[
            pl.BlockSpec((D_MODEL, PEER_TM), lambda i, e: (0, i)),
            pl.BlockSpec((PEER_TM, D_MODEL), lambda i, e: (i, 0)),
            pl.BlockSpec((2 * PEER_HEADS * PEER_HALF, D_MODEL), lambda i, e: (0, 0)),
            pl.BlockSpec((2 * PEER_HEADS, PEER_NKEYS, PEER_HALF), lambda i, e: (0, 0, 0)),
            pl.BlockSpec((PEER_TE, D_MODEL), lambda i, e: (e, 0)),
            pl.BlockSpec((D_MODEL, PEER_TE), lambda i, e: (0, e)),
        ],
        out_specs=pl.BlockSpec((PEER_TM, D_MODEL), lambda i, e: (i, 0)),
        out_shape=jax.ShapeDtypeStruct((T, D_MODEL), F32),
        scratch_shapes=[head_bf16, head_bf16, head_f32, head_f32,
                        pltpu.VMEM((D_MODEL, PEER_TM), F32),
                        pltpu.VMEM((PEER_TE, PEER_TM), BF16)],
        compiler_params=pltpu.CompilerParams(
            dimension_semantics=("arbitrary", "arbitrary"), vmem_limit_bytes=VMEM_LIMIT),
        name="peer",
    )(xn_t, x2, wq_t, sub_keys, u, v_t)


def _rope_tables(S):
    pos = jnp.arange(S, dtype=F32)
    inv = 1.0 / (ROPE_THETA ** (jnp.arange(0, HEAD_DIM, 2, dtype=F32) / HEAD_DIM))
    ang = pos[:, None] * inv[None, :]
    cos, sin = jnp.cos(ang), jnp.sin(ang)
    reps = MXU_DIM // HEAD_DIM
    cosf = jnp.tile(jnp.concatenate([cos, cos], axis=1), (1, reps))
    sinf = jnp.tile(jnp.concatenate([-sin, sin], axis=1), (1, reps))
    return cosf, sinf


def _constants(nb):
    idx = np.arange(MXU_DIM)
    bd = (idx[:, None] // HEAD_DIM == idx[None, :] // HEAD_DIM).astype(np.float32) / HEAD_DIM
    rows = np.arange(nb * MOBA_HEADS)
    cols = np.arange(MOBA_WIDTH)
    headmask = (cols[None, :] // HEAD_DIM == rows[:, None] % MOBA_HEADS).astype(np.float32)
    return jnp.asarray(bd, BF16), jnp.asarray(headmask)


def kernel(x, attn_norm, w_in, q_norm_moba, k_norm_moba, q_norm_diff, k_norm_diff, lambda_q1, lambda_k1, lambda_q2, lambda_k2, moba_out_gain, diff_out_gain, w_out, ffn_norm, peer_query, peer_sub_keys, peer_down, peer_up):
    B, S, D = x.shape
    assert D == D_MODEL and S % MOBA_BLOCK == 0 and (B * S) % PEER_TM == 0
    nb = S // MOBA_BLOCK
    T = B * S
    x2d = x.reshape(T, D)
    cosf, sinf = _rope_tables(S)
    bd, headmask = _constants(nb)

    i = 0
    w = w_in[i]
    mqw, mkw, mvw, dqw, dkw, dvw = jnp.split(w, [512, 1024, 1536, 2048, 2560], axis=1)
    w_qk = jnp.concatenate([mqw, mkw, dqw, dkw], axis=1).astype(BF16)
    w_v_t = jnp.concatenate([mvw, dvw], axis=1).T.astype(BF16)
    scale = HEAD_DIM ** -0.5 * math.log2(math.e)
    qk_gain = jnp.concatenate([
        jnp.tile(q_norm_moba[i], MOBA_HEADS) * scale,
        jnp.tile(k_norm_moba[i], MOBA_HEADS),
        jnp.tile(q_norm_diff[i].reshape(-1), DIFF_HEADS) * scale,
        jnp.tile(k_norm_diff[i].reshape(-1), DIFF_HEADS),
    ]).reshape(1, QK_COLS).astype(F32)

    proj, v_t, sel = _in_proj(x2d, attn_norm[i].reshape(1, D), w_qk, w_v_t, qk_gain, cosf, sinf,
                              bd, headmask, nb)
    a_out = _attention(proj, v_t, sel, moba_out_gain[i].reshape(1, MOBA_WIDTH), B, S, moba=True)
    lam_params = jnp.stack([lambda_q1[i], lambda_k1[i], lambda_q2[i], lambda_k2[i]]).astype(F32)
    b_out = _attention(proj, v_t, lam_params, diff_out_gain[i].reshape(1, DIFF_WIDTH), B, S, moba=False)

    w_o = w_out[i].astype(BF16)
    x2, xn_t = _out_proj(x2d, a_out, b_out, w_o[:MOBA_WIDTH], w_o[MOBA_WIDTH:], ffn_norm[i].reshape(1, D))

    wq_t = peer_query[i].T.astype(BF16)
    sub_keys = peer_sub_keys[i].reshape(2 * PEER_HEADS, PEER_NKEYS, PEER_HALF).astype(BF16)
    u = peer_down[i].astype(BF16)
    v_up_t = (0.5 * peer_up[i]).T.astype(BF16)
    out = _peer(xn_t, x2, wq_t, sub_keys, u, v_up_t)
    return out.reshape(B, S, D)
```
